```python
import math
import jax, jax.numpy as jnp
from jax import lax
import numpy as np

D_MODEL = 1024
BATCH = 32
SEQ = 2048
DEPTH = 1

HEAD_DIM = 64
HEADS_PER_GROUP = 8
DILATED_GROUPS = ((128, 1), (512, 4), (2048, 16))
N_GROUPS = len(DILATED_GROUPS)
N_ATTN_HEADS = N_GROUPS * HEADS_PER_GROUP
ATTN_QKV = N_ATTN_HEADS * HEAD_DIM
ATTN_OUT = HEADS_PER_GROUP * HEAD_DIM
BLOCK = 128
NEG = -1e30
NUM_BUCKETS = 32
MAX_EXACT = NUM_BUCKETS // 2
MAX_DISTANCE = 2048
LRU_WIDTH = D_MODEL
LRU_BLOCKS = 16
LRU_BLOCK_W = LRU_WIDTH // LRU_BLOCKS
LRU_CONV = 4
C_RGLRU = 8.0
D_FF = 3 * D_MODEL
FFN_CONV = 3
IN_COLS = 3 * ATTN_QKV + 2 * LRU_WIDTH + 2 * D_MODEL
ALPHA = (2.0 * DEPTH) ** 0.25
BETA = (8.0 * DEPTH) ** -0.25
LN_EPS = 1e-5

kernel_name = "hybrid_dilated_attn_rglru_convffn_block"


def layer_norm(x, g, b):
    xf = x.astype(jnp.float32)
    mu = xf.mean(-1, keepdims=True)
    var = jnp.square(xf - mu).mean(-1, keepdims=True)
    return ((xf - mu) * lax.rsqrt(var + LN_EPS) * g + b).astype(x.dtype)


def causal_dwconv(x, w, b):
    K = w.shape[0]
    S = x.shape[1]
    xp = jnp.pad(x, ((0, 0), (K - 1, 0), (0, 0)))
    y = b
    for k in range(K):
        y = y + xp[:, k:k + S] * w[k]
    return y


def rel_bucket(dist):
    is_small = dist < MAX_EXACT
    nf = jnp.maximum(dist, 1).astype(jnp.float32)
    large = MAX_EXACT + (jnp.log(nf / MAX_EXACT) / math.log(MAX_DISTANCE / MAX_EXACT)
                         * (NUM_BUCKETS - MAX_EXACT)).astype(jnp.int32)
    large = jnp.minimum(large, NUM_BUCKETS - 1)
    return jnp.where(is_small, dist, large)


def dilated_window_attention(q, k, v, bias_table, window, dilation):
    B, S, H, Dh = q.shape
    span = dilation * BLOCK
    Sp = -(-S // span) * span
    L = Sp // dilation
    nblk = L // BLOCK

    def to_blocks(t):
        t = jnp.pad(t, ((0, 0), (0, Sp - S), (0, 0), (0, 0)))
        t = t.reshape(B, L, dilation, H, Dh).transpose(0, 2, 3, 1, 4)
        return t.reshape(B, dilation, H, nblk, BLOCK, Dh)

    def with_prev(t):
        prev = jnp.pad(t, ((0, 0), (0, 0), (0, 0), (1, 0), (0, 0), (0, 0)))[:, :, :, :-1]
        return jnp.concatenate([prev, t], axis=4)

    qb = to_blocks(q * (HEAD_DIM ** -0.5))
    kk = with_prev(to_blocks(k))
    vv = with_prev(to_blocks(v))
    logits = jnp.einsum('brhnqc,brhnkc->brhnqk', qb, kk).astype(jnp.float32)

    qi = jnp.arange(BLOCK)[:, None]
    kj = jnp.arange(2 * BLOCK)[None, :]
    dist = qi + BLOCK - kj
    in_band = (dist >= 0) & (dist <= window // dilation)
    bucket = rel_bucket(jnp.maximum(dist, 0) * dilation)
    bias = bias_table[bucket].astype(jnp.float32).transpose(2, 0, 1)
    blk = jnp.arange(nblk)[:, None, None]
    mask = in_band[None] & ((blk > 0) | (kj >= BLOCK)[None])

    logits = jnp.where(mask, logits + bias[:, None], NEG)
    m = logits.max(-1, keepdims=True)
    p = jnp.exp(logits - m)
    s = p.sum(-1, keepdims=True)
    o = jnp.einsum('brhnqk,brhnkc->brhnqc', p, vv.astype(jnp.float32)) / s
    lse = (m + jnp.log(s))[..., 0]
    o = o.reshape(B, dilation, H, L, Dh).transpose(0, 3, 1, 2, 4).reshape(B, Sp, H, Dh)[:, :S]
    lse = lse.reshape(B, dilation, H, L).transpose(0, 3, 1, 2).reshape(B, Sp, H)[:, :S]
    return o, lse


def rg_lru(xr, wa, ba, wx, bx, lam):
    B, S, W = xr.shape
    xf = xr.astype(jnp.float32)
    xb = xf.reshape(B, S, LRU_BLOCKS, LRU_BLOCK_W)
    ba_b = ba.astype(jnp.float32).reshape(LRU_BLOCKS, LRU_BLOCK_W)
    bx_b = bx.astype(jnp.float32).reshape(LRU_BLOCKS, LRU_BLOCK_W)
    r = jax.nn.sigmoid(jnp.einsum('bsnc,ncd->bsnd', xb, wa.astype(jnp.float32)) + ba_b).reshape(B, S, W)
    i = jax.nn.sigmoid(jnp.einsum('bsnc,ncd->bsnd', xb, wx.astype(jnp.float32)) + bx_b).reshape(B, S, W)
    log_a = -C_RGLRU * r * jax.nn.softplus(-lam.astype(jnp.float32))
    a = jnp.exp(log_a)
    b = jnp.sqrt(-jnp.expm1(2.0 * log_a)) * (i * xf)

    def combine(left, right):
        a1, b1 = left
        a2, b2 = right
        return a1 * a2, a2 * b1 + b2

    _, h = lax.associative_scan(combine, (a, b), axis=1)
    return h.astype(xr.dtype)


def setup_inputs(seed: int = 0) -> dict:
    key = jax.random.key(seed)
    ks = jax.random.split(key, 24)
    f32 = jnp.float32
    nrm = lambda k, shape, scale: jax.random.normal(k, shape, f32) * scale
    D = D_MODEL
    w_in = nrm(ks[2], (DEPTH, D, IN_COLS), D ** -0.5)
    w_in = w_in.at[:, :, 2 * ATTN_QKV:3 * ATTN_QKV].multiply(BETA)
    a_c = jax.random.uniform(ks[10], (DEPTH, LRU_WIDTH), f32, 0.9, 0.999)
    a0 = a_c ** (1.0 / C_RGLRU)
    lam = jnp.log(a0) - jnp.log1p(-a0)
    return {
        "x": nrm(ks[0], (BATCH, SEQ, D), 1.0),
        "c": nrm(ks[1], (BATCH, D), 1.0),
        "w_ada": nrm(ks[3], (DEPTH, D, 6 * D), D ** -0.5),
        "b_ada": nrm(ks[4], (DEPTH, 6 * D), 0.01),
        "w_in": w_in,
        "rel_bias": nrm(ks[5], (NUM_BUCKETS, N_ATTN_HEADS), 0.5),
        "lru_conv_w": nrm(ks[6], (DEPTH, LRU_CONV, LRU_WIDTH), LRU_CONV ** -0.5),
        "lru_conv_b": nrm(ks[7], (DEPTH, LRU_WIDTH), 0.01),
        "lru_wa": nrm(ks[8], (DEPTH, LRU_BLOCKS, LRU_BLOCK_W, LRU_BLOCK_W), LRU_BLOCK_W ** -0.5),
        "lru_ba": nrm(ks[9], (DEPTH, LRU_WIDTH), 0.01),
        "lru_wx": nrm(ks[11], (DEPTH, LRU_BLOCKS, LRU_BLOCK_W, LRU_BLOCK_W), LRU_BLOCK_W ** -0.5),
        "lru_bx": nrm(ks[12], (DEPTH, LRU_WIDTH), 0.01),
        "lru_lambda": lam,
        "w_proj_attn": nrm(ks[13], (DEPTH, ATTN_OUT, D), BETA * ATTN_OUT ** -0.5),
        "w_proj_lru": nrm(ks[14], (DEPTH, LRU_WIDTH, D), BETA * LRU_WIDTH ** -0.5),
        "w_out": nrm(ks[15], (DEPTH, D, D), BETA * D ** -0.5),
        "ln1_g": 1.0 + nrm(ks[16], (DEPTH, D), 0.02),
        "ln1_b": nrm(ks[17], (DEPTH, D), 0.02),
        "ffn_w_up": nrm(ks[18], (DEPTH, D, 2 * D_FF), D ** -0.5),
        "ffn_conv_w": nrm(ks[19], (DEPTH, FFN_CONV, 2 * D_FF), FFN_CONV ** -0.5),
        "ffn_conv_b": nrm(ks[20], (DEPTH, 2 * D_FF), 0.01),
        "ffn_w_down": nrm(ks[21], (DEPTH, D_FF, D), BETA * D_FF ** -0.5),
        "ln2_g": 1.0 + nrm(ks[22], (DEPTH, D), 0.02),
        "ln2_b": nrm(ks[23], (DEPTH, D), 0.02),
    }


def reference(x, c, w_ada, b_ada, w_in, rel_bias, lru_conv_w, lru_conv_b, lru_wa, lru_ba,
              lru_wx, lru_bx, lru_lambda, w_proj_attn, w_proj_lru, w_out, ln1_g, ln1_b,
              ffn_w_up, ffn_conv_w, ffn_conv_b, ffn_w_down, ln2_g, ln2_b):
    B, S, D = x.shape
    c_act = jax.nn.silu(c)
    split_idx = [ATTN_QKV, 2 * ATTN_QKV, 3 * ATTN_QKV, 3 * ATTN_QKV + LRU_WIDTH,
                 3 * ATTN_QKV + 2 * LRU_WIDTH, 3 * ATTN_QKV + 2 * LRU_WIDTH + D_MODEL]
    for l in range(DEPTH):
        mod = c_act @ w_ada[l] + b_ada[l]
        sh1, sc1, g1, sh2, sc2, g2 = [t[:, None, :] for t in jnp.split(mod, 6, axis=-1)]

        u = x * (1.0 + sc1) + sh1
        proj = u @ w_in[l]
        q, k, v, x_lru, lru_gate, gate_a, gate_r = jnp.split(proj, split_idx, axis=-1)
        q = q.reshape(B, S, N_GROUPS, HEADS_PER_GROUP, HEAD_DIM)
        k = k.reshape(B, S, N_GROUPS, HEADS_PER_GROUP, HEAD_DIM)
        v = v.reshape(B, S, N_GROUPS, HEADS_PER_GROUP, HEAD_DIM)
        outs, lses = [], []
        for g, (window, dilation) in enumerate(DILATED_GROUPS):
            o_g, lse_g = dilated_window_attention(
                q[:, :, g], k[:, :, g], v[:, :, g],
                rel_bias[:, g * HEADS_PER_GROUP:(g + 1) * HEADS_PER_GROUP], window, dilation)
            outs.append(o_g)
            lses.append(lse_g)
        wts = jax.nn.softmax(jnp.stack(lses, 0), axis=0)
        o_attn = jnp.sum(wts[..., None] * jnp.stack(outs, 0), axis=0)
        y_attn = o_attn.reshape(B, S, ATTN_OUT).astype(x.dtype) @ w_proj_attn[l]

        xr = causal_dwconv(x_lru, lru_conv_w[l], lru_conv_b[l])
        h = rg_lru(xr, lru_wa[l], lru_ba[l], lru_wx[l], lru_bx[l], lru_lambda[l])
        y_lru = (h * jax.nn.gelu(lru_gate)) @ w_proj_lru[l]

        merged = jax.nn.sigmoid(gate_a) * y_attn + jax.nn.sigmoid(gate_r) * y_lru
        mix_out = merged @ w_out[l]
        x = layer_norm(ALPHA * x + g1 * mix_out, ln1_g[l], ln1_b[l])

        u2 = x * (1.0 + sc2) + sh2
        hff = causal_dwconv(u2 @ ffn_w_up[l], ffn_conv_w[l], ffn_conv_b[l])
        val, gt = jnp.split(hff, 2, axis=-1)
        ffn_out = (jax.nn.gelu(gt) * val) @ ffn_w_down[l]
        x = layer_norm(ALPHA * x + g2 * ffn_out, ln2_g[l], ln2_b[l])
    return x
```

```python
import functools
import math

import jax
import jax.numpy as jnp
from jax import lax
from jax.experimental import pallas as pl
from jax.experimental.pallas import tpu as pltpu

F32 = jnp.float32
BF16 = jnp.bfloat16

D_MODEL = 1024
HEAD_DIM = 64
HEADS_PER_GROUP = 8
DILATED_GROUPS = ((128, 1), (512, 4), (2048, 16))
N_GROUPS = len(DILATED_GROUPS)
GROUP_W = HEADS_PER_GROUP * HEAD_DIM
ATTN_QKV = N_GROUPS * GROUP_W
BLOCK = 128
NEG = -1e30
NUM_BUCKETS = 32
MAX_EXACT = NUM_BUCKETS // 2
MAX_DISTANCE = 2048
LRU_WIDTH = D_MODEL
LRU_BLOCKS = 16
LRU_BLOCK_W = LRU_WIDTH // LRU_BLOCKS
LRU_CONV = 4
C_RGLRU = 8.0
D_FF = 3 * D_MODEL
FFN_CONV = 3
DEPTH = 1
ALPHA = (2.0 * DEPTH) ** 0.25
LN_EPS = 1e-5

LANES = 128
SUBLANES = 8
VMEM_LIMIT = 56 * 1024 * 1024
TM_IN = 512
TM_MIX = 512
TM_FFN = 512
TS_LRU = 256
CH_IN = 512
CH_FF = 512
LRU_GRP = 256
RES_PER_STEP = (1, 1, 4)
HALO = SUBLANES


def _cparams(sem):
    return pltpu.CompilerParams(dimension_semantics=sem, vmem_limit_bytes=VMEM_LIMIT)


def _resident(shape, index_map):
    return pl.BlockSpec(shape, index_map, pipeline_mode=pl.Buffered(1))


def _gelu_tanh(x):
    return 0.5 * x * (1.0 + jnp.tanh(math.sqrt(2.0 / math.pi) * (x + 0.044715 * (x * x * x))))


def _sigmoid(x):
    return 1.0 / (1.0 + jnp.exp(-x))


def _layer_norm(z, g, b):
    mu = jnp.mean(z, axis=-1, keepdims=True)
    zc = z - mu
    var = jnp.mean(zc * zc, axis=-1, keepdims=True)
    return zc * lax.rsqrt(var + LN_EPS) * g + b


def _ada_kernel(c_ref, w_ref, b_ref, o_ref):
    c = c_ref[...]
    ca = (c * _sigmoid(c)).astype(BF16)
    o_ref[...] = jnp.dot(ca, w_ref[...].astype(BF16), preferred_element_type=F32) + b_ref[...]


def _ada(c, w_ada, b_ada):
    B, D = c.shape
    N = w_ada.shape[1]
    tn = D
    return pl.pallas_call(
        _ada_kernel,
        grid=(N // tn,),
        in_specs=[pl.BlockSpec((B, D), lambda j: (0, 0)),
                  pl.BlockSpec((D, tn), lambda j: (0, j)),
                  pl.BlockSpec((1, tn), lambda j: (0, j))],
        out_specs=pl.BlockSpec((B, tn), lambda j: (0, j)),
        out_shape=jax.ShapeDtypeStruct((B, N), F32),
        compiler_params=_cparams(("arbitrary",)),
        name="ada",
    )(c, w_ada, b_ada.reshape(1, N))


def _inproj_kernel(x_ref, mod_ref, w_ref, q_ref, k_ref, v_ref, xl_ref, gl_ref, ga_ref, gr_ref):
    sh1 = mod_ref[0:1, :]
    sc1 = mod_ref[1:2, :]
    u = (x_ref[...] * (1.0 + sc1) + sh1).astype(BF16)

    def mm(c0):
        return jnp.dot(u, w_ref[:, c0:c0 + CH_IN], preferred_element_type=F32)

    per_grp = GROUP_W // CH_IN
    col = 0
    for g in range(N_GROUPS):
        for j in range(per_grp):
            q_ref[g, :, j * CH_IN:(j + 1) * CH_IN] = (mm(col) * (HEAD_DIM ** -0.5)).astype(BF16)
            col += CH_IN
    for ref in (k_ref, v_ref):
        for g in range(N_GROUPS):
            for j in range(per_grp):
                ref[g, :, j * CH_IN:(j + 1) * CH_IN] = mm(col).astype(BF16)
                col += CH_IN
    for j in range(LRU_WIDTH // CH_IN):
        xl_ref[:, j * CH_IN:(j + 1) * CH_IN] = mm(col)
        col += CH_IN
    for j in range(LRU_WIDTH // CH_IN):
        gl_ref[:, j * CH_IN:(j + 1) * CH_IN] = _gelu_tanh(mm(col)).astype(BF16)
        col += CH_IN
    for ref in (ga_ref, gr_ref):
        for j in range(D_MODEL // CH_IN):
            ref[:, j * CH_IN:(j + 1) * CH_IN] = _sigmoid(mm(col)).astype(BF16)
            col += CH_IN


def _inproj(x, mod, w_in_b):
    B, S, D = x.shape
    tm = TM_IN
    ncols = w_in_b.shape[1]
    row = lambda b, i: (b, i, 0)
    grp = lambda b, i: (0, b, i, 0)
    qkv_shape = jax.ShapeDtypeStruct((N_GROUPS, B, S, GROUP_W), BF16)
    qkv_spec = pl.BlockSpec((N_GROUPS, None, tm, GROUP_W), grp)
    bf_shape = jax.ShapeDtypeStruct((B, S, D), BF16)
    row_spec = pl.BlockSpec((None, tm, D), row)
    return pl.pallas_call(
        _inproj_kernel,
        grid=(B, S // tm),
        in_specs=[row_spec,
                  pl.BlockSpec((None, 6, D), lambda b, i: (b, 0, 0)),
                  _resident((D, ncols), lambda b, i: (0, 0))],
        out_specs=[qkv_spec, qkv_spec, qkv_spec, row_spec, row_spec, row_spec, row_spec],
        out_shape=[qkv_shape, qkv_shape, qkv_shape,
                   jax.ShapeDtypeStruct((B, S, LRU_WIDTH), F32), bf_shape, bf_shape, bf_shape],
        compiler_params=_cparams(("parallel", "parallel")),
        name="inproj",
    )(x, mod, w_in_b)


def _bias_kernel(tab_ref, bucket_ref, o_ref):
    hh = pl.program_id(0)
    bucket = bucket_ref[...]
    acc = jnp.zeros(bucket.shape, F32)
    for b in range(NUM_BUCKETS):
        acc = jnp.where(bucket == b, tab_ref[b, hh], acc)
    o_ref[...] = acc


def _bias_tiles(rel_bias, buckets):
    nh = rel_bias.shape[1]
    return pl.pallas_call(
        _bias_kernel,
        grid=(nh,),
        in_specs=[pl.BlockSpec(memory_space=pltpu.SMEM),
                  pl.BlockSpec((None, BLOCK, 2 * BLOCK), lambda h: (h // HEADS_PER_GROUP, 0, 0))],
        out_specs=pl.BlockSpec((None, BLOCK, 2 * BLOCK), lambda h: (h, 0, 0)),
        out_shape=jax.ShapeDtypeStruct((nh, BLOCK, 2 * BLOCK), F32),
        compiler_params=_cparams(("arbitrary",)),
        name="bias_tiles",
    )(rel_bias, buckets)


def _bucket_maps():
    qi = jnp.arange(BLOCK)[:, None]
    kj = jnp.arange(2 * BLOCK)[None, :]
    dist = jnp.maximum(qi + BLOCK - kj, 0)
    maps = []
    for _, dil in DILATED_GROUPS:
        d = dist * dil
        nf = jnp.maximum(d, 1).astype(F32)
        large = MAX_EXACT + (jnp.log(nf / MAX_EXACT) / math.log(MAX_DISTANCE / MAX_EXACT)
                             * (NUM_BUCKETS - MAX_EXACT)).astype(jnp.int32)
        large = jnp.minimum(large, NUM_BUCKETS - 1)
        maps.append(jnp.where(d < MAX_EXACT, d, large))
    return jnp.stack(maps, 0).astype(jnp.int32)


def _attn_kernel(*refs, nblk, res, has_prev, last):
    q_ref, k_ref, v_ref, bias_ref = refs[:4]
    pos = 4
    if has_prev:
        po_ref, pst_ref = refs[4:6]
        pos = 6
    o_ref = refs[pos]
    st_ref = None if last else refs[pos + 1]

    pair_w = 2 * HEAD_DIM
    n_pairs = GROUP_W // pair_w
    lane = lax.broadcasted_iota(jnp.int32, (BLOCK, pair_w), 1)
    first_head = lane < HEAD_DIM
    qi = lax.broadcasted_iota(jnp.int32, (2 * BLOCK, 2 * BLOCK), 0) & (BLOCK - 1)
    kj = lax.broadcasted_iota(jnp.int32, (2 * BLOCK, 2 * BLOCK), 1)
    dist = qi + BLOCK - kj
    band = (dist >= 0) & (dist <= BLOCK)
    qi0 = lax.broadcasted_iota(jnp.int32, (2 * BLOCK, BLOCK), 0) & (BLOCK - 1)
    kj0 = lax.broadcasted_iota(jnp.int32, (2 * BLOCK, BLOCK), 1)
    band0 = qi0 >= kj0

    def do_block(r0, first):
        for rr in range(res):
            stats = jnp.zeros((BLOCK, LANES), F32)
            if has_prev:
                pst = pst_ref[pl.ds(r0, BLOCK), rr * LANES:(rr + 1) * LANES]
            for hp in range(n_pairs):
                c = rr * GROUP_W + hp * pair_w
                qp = q_ref[pl.ds(r0, BLOCK), c:c + pair_w]
                zero = jnp.zeros_like(qp)
                q2 = jnp.concatenate([jnp.where(first_head, qp, zero),
                                      jnp.where(first_head, zero, qp)], axis=0)
                if first:
                    kp = k_ref[pl.ds(r0, BLOCK), c:c + pair_w]
                    vp = v_ref[pl.ds(r0, BLOCK), c:c + pair_w]
                    bias2 = jnp.concatenate([bias_ref[2 * hp, :, BLOCK:], bias_ref[2 * hp + 1, :, BLOCK:]], axis=0)
                    msk = band0
                else:
                    k0 = pl.multiple_of(r0 - BLOCK, BLOCK)
                    kp = k_ref[pl.ds(k0, 2 * BLOCK), c:c + pair_w]
                    vp = v_ref[pl.ds(k0, 2 * BLOCK), c:c + pair_w]
                    bias2 = jnp.concatenate([bias_ref[2 * hp], bias_ref[2 * hp + 1]], axis=0)
                    msk = band
                logits = lax.dot_general(q2, kp, (((1,), (1,)), ((), ())), preferred_element_type=F32)
                logits = jnp.where(msk, logits + bias2, NEG)
                m = jnp.max(logits, axis=-1, keepdims=True)
                p = jnp.exp(logits - m)
                s = jnp.sum(p, axis=-1, keepdims=True)
                pv = jnp.dot(p.astype(BF16), vp, preferred_element_type=F32)
                on = pv / s
                lse = m + jnp.log(s)
                lse_a, lse_b = lse[:BLOCK], lse[BLOCK:]
                o_a, o_b = on[:BLOCK], on[BLOCK:]
                if has_prev:
                    pl_a = pst[:, 2 * hp:2 * hp + 1]
                    pl_b = pst[:, 2 * hp + 1:2 * hp + 2]
                    po = po_ref[pl.ds(r0, BLOCK), c:c + pair_w]
                    mx_a = jnp.maximum(pl_a, lse_a)
                    mx_b = jnp.maximum(pl_b, lse_b)
                    ep_a, ec_a = jnp.exp(pl_a - mx_a), jnp.exp(lse_a - mx_a)
                    ep_b, ec_b = jnp.exp(pl_b - mx_b), jnp.exp(lse_b - mx_b)
                    den_a, den_b = ep_a + ec_a, ep_b + ec_b
                    w_prev = jnp.where(first_head, ep_a / den_a, ep_b / den_b)
                    o_pair = w_prev * po + jnp.where(first_head, (ec_a / den_a) * o_a, (ec_b / den_b) * o_b)
                    lse_a = mx_a + jnp.log(den_a)
                    lse_b = mx_b + jnp.log(den_b)
                else:
                    o_pair = jnp.where(first_head, o_a, o_b)
                o_ref[pl.ds(r0, BLOCK), c:c + pair_w] = o_pair.astype(o_ref.dtype)
                if not last:
                    stats = jnp.where(lane == 2 * hp, lse_a, stats)
                    stats = jnp.where(lane == 2 * hp + 1, lse_b, stats)
            if not last:
                st_ref[pl.ds(r0, BLOCK), rr * LANES:(rr + 1) * LANES] = stats

    do_block(0, True)
    if nblk > 1:
        def body(n, carry):
            do_block(pl.multiple_of(n * BLOCK, BLOCK), False)
            return carry
        lax.fori_loop(1, nblk, body, 0)


def _attn_group(g, q, k, v, bias, prev):
    _, B, S, _ = q.shape
    _, dil = DILATED_GROUPS[g]
    L = S // dil
    nblk = L // BLOCK
    res = RES_PER_STEP[g]
    last = g == N_GROUPS - 1
    has_prev = prev is not None
    view = lambda t: t.reshape(N_GROUPS, B, L, dil * GROUP_W)
    qkv_spec = pl.BlockSpec((None, None, L, res * GROUP_W), lambda b, j: (g, b, 0, j))
    o_spec = pl.BlockSpec((None, L, res * GROUP_W), lambda b, j: (b, 0, j))
    st_spec = pl.BlockSpec((None, L, res * LANES), lambda b, j: (b, 0, j))
    hp = HEADS_PER_GROUP
    in_specs = [qkv_spec, qkv_spec, qkv_spec,
                _resident((hp, BLOCK, 2 * BLOCK), lambda b, j: (g, 0, 0))]
    args = [view(q), view(k), view(v), bias]
    if has_prev:
        in_specs += [o_spec, st_spec]
        args += [prev[0].reshape(B, L, dil * GROUP_W), prev[1].reshape(B, L, dil * LANES)]
    o_shape = jax.ShapeDtypeStruct((B, L, dil * GROUP_W), BF16 if last else F32)
    st_shape = jax.ShapeDtypeStruct((B, L, dil * LANES), F32)
    out = pl.pallas_call(
        functools.partial(_attn_kernel, nblk=nblk, res=res, has_prev=has_prev, last=last),
        grid=(B, dil // res),
        in_specs=in_specs,
        out_specs=o_spec if last else [o_spec, st_spec],
        out_shape=o_shape if last else [o_shape, st_shape],
        compiler_params=_cparams(("parallel", "parallel")),
        name=f"attn_g{g}",
    )(*args)
    if last:
        return out.reshape(B, S, GROUP_W)
    return out[0].reshape(B, S, GROUP_W), out[1].reshape(B, S, LANES)


def _lru_kernel(xl_ref, gl_ref, cw_ref, cb_ref, wg_ref, ba_ref, bx_ref, lam_ref, o_ref, xpad_sc, hc_sc):
    ts = xl_ref.shape[0]
    W = xl_ref.shape[1]

    @pl.when(pl.program_id(1) == 0)
    def _():
        xpad_sc[0:HALO, :] = jnp.zeros((HALO, W), F32)
        hc_sc[...] = jnp.zeros(hc_sc.shape, F32)

    xpad_sc[HALO:HALO + ts, :] = xl_ref[...]
    xr = cb_ref[...] + cw_ref[0:1, :] * xpad_sc[pl.ds(HALO - LRU_CONV + 1, ts), :]
    for kk in range(1, LRU_CONV):
        xr = xr + cw_ref[kk:kk + 1, :] * xpad_sc[pl.ds(HALO - LRU_CONV + 1 + kk, ts), :]
    xpad_sc[0:HALO, :] = xl_ref[ts - HALO:ts, :]

    xb = xr.astype(BF16)
    ngrp = W // LRU_GRP
    zs = [jnp.dot(xb[:, c * LRU_GRP:(c + 1) * LRU_GRP], wg_ref[c], preferred_element_type=F32)
          for c in range(ngrp)]
    za = jnp.concatenate([z[:, :LRU_GRP] for z in zs], axis=1)
    zx = jnp.concatenate([z[:, LRU_GRP:] for z in zs], axis=1)
    r = _sigmoid(za + ba_ref[...])
    gi = _sigmoid(zx + bx_ref[...])
    nl = -lam_ref[...]
    softplus = jnp.maximum(nl, 0.0) + jnp.log1p(jnp.exp(-jnp.abs(nl)))
    log_a = (-C_RGLRU) * r * softplus
    a = jnp.exp(log_a)
    bt = jnp.sqrt(-jnp.tanh(log_a) * (a * a + 1.0)) * (gi * xr)

    row = lax.broadcasted_iota(jnp.int32, (ts, W), 0)
    sh = 1
    while sh < ts:
        if sh < SUBLANES:
            a_sh = jnp.where(row >= sh, pltpu.roll(a, sh, 0), 1.0)
            b_sh = jnp.where(row >= sh, pltpu.roll(bt, sh, 0), 0.0)
        else:
            a_sh = jnp.concatenate([jnp.ones((sh, W), F32), a[:ts - sh]], axis=0)
            b_sh = jnp.concatenate([jnp.zeros((sh, W), F32), bt[:ts - sh]], axis=0)
        bt = a * b_sh + bt
        a = a * a_sh
        sh *= 2
    h = a * hc_sc[0:1, :] + bt
    hc_sc[...] = jnp.broadcast_to(h[ts - 1:ts, :], hc_sc.shape)
    o_ref[...] = (h * gl_ref[...].astype(F32)).astype(BF16)


def _lru(x_lru, gl, conv_w, conv_b, wg, ba, bx, lam):
    B, S, W = x_lru.shape
    ts = TS_LRU
    row = lambda b, i: (b, i, 0)
    const2 = lambda b, i: (0, 0)
    vec = pl.BlockSpec((1, W), const2)
    return pl.pallas_call(
        _lru_kernel,
        grid=(B, S // ts),
        in_specs=[pl.BlockSpec((None, ts, W), row), pl.BlockSpec((None, ts, W), row),
                  pl.BlockSpec((LRU_CONV, W), const2), vec,
                  pl.BlockSpec(wg.shape, lambda b, i: (0, 0, 0)), vec, vec, vec],
        out_specs=pl.BlockSpec((None, ts, W), row),
        out_shape=jax.ShapeDtypeStruct((B, S, W), BF16),
        scratch_shapes=[pltpu.VMEM((ts + HALO, W), F32), pltpu.VMEM((SUBLANES, W), F32)],
        compiler_params=_cparams(("parallel", "arbitrary")),
        name="lru",
    )(x_lru, gl, conv_w, conv_b.reshape(1, W), wg, ba.reshape(1, W), bx.reshape(1, W), lam.reshape(1, W))


def _lru_gate_weights(wa, wx):
    per = LRU_GRP // LRU_BLOCK_W
    ngrp = LRU_BLOCKS // per

    def bd(w):
        w = w.reshape(ngrp, per, LRU_BLOCK_W, LRU_BLOCK_W)
        eye = jnp.eye(per, dtype=w.dtype)
        t = jnp.einsum('gpcd,pq->gpcqd', w, eye)
        return t.reshape(ngrp, LRU_GRP, LRU_GRP)

    return jnp.concatenate([bd(wa), bd(wx)], axis=2).astype(BF16)


def _mix_kernel(x_ref, oa_ref, hg_ref, ga_ref, gr_ref, mod_ref, wpa_ref, wpl_ref, wo_ref,
                g_ref, b_ref, o_ref):
    ya = jnp.dot(oa_ref[...], wpa_ref[...], preferred_element_type=F32)
    yl = jnp.dot(hg_ref[...], wpl_ref[...], preferred_element_type=F32)
    merged = ga_ref[...].astype(F32) * ya + gr_ref[...].astype(F32) * yl
    mo = jnp.dot(merged.astype(BF16), wo_ref[...], preferred_element_type=F32)
    g1 = mod_ref[2:3, :]
    z = ALPHA * x_ref[...] + g1 * mo
    o_ref[...] = _layer_norm(z, g_ref[...], b_ref[...])


def _mix(x, o_attn, hg, ga, gr, mod, wpa, wpl, wo, ln_g, ln_b):
    B, S, D = x.shape
    tm = TM_MIX
    row = lambda b, i: (b, i, 0)
    const2 = lambda b, i: (0, 0)
    rspec = lambda w: pl.BlockSpec((None, tm, w), row)
    vec = pl.BlockSpec((1, D), const2)
    return pl.pallas_call(
        _mix_kernel,
        grid=(B, S // tm),
        in_specs=[rspec(D), rspec(GROUP_W), rspec(LRU_WIDTH), rspec(D), rspec(D),
                  pl.BlockSpec((None, 6, D), lambda b, i: (b, 0, 0)),
                  _resident(wpa.shape, const2), _resident(wpl.shape, const2), _resident(wo.shape, const2),
                  vec, vec],
        out_specs=rspec(D),
        out_shape=jax.ShapeDtypeStruct((B, S, D), F32),
        compiler_params=_cparams(("parallel", "parallel")),
        name="mix",
    )(x, o_attn, hg, ga, gr, mod, wpa, wpl, wo, ln_g.reshape(1, D), ln_b.reshape(1, D))


def _ffn_kernel(x_ref, mod_ref, wup_ref, cw_ref, cb_ref, wdn_ref, g_ref, b_ref, o_ref, hpad_sc, carry_sc):
    tm = x_ref.shape[0]
    nch = wup_ref.shape[0]

    @pl.when(pl.program_id(1) == 0)
    def _():
        carry_sc[...] = jnp.zeros(carry_sc.shape, F32)

    x1 = x_ref[...]
    sh2 = mod_ref[3:4, :]
    sc2 = mod_ref[4:5, :]
    g2 = mod_ref[5:6, :]
    u2 = (x1 * (1.0 + sc2) + sh2).astype(BF16)
    acc = jnp.zeros((tm, D_MODEL), F32)
    for c in range(nch):
        hcat = jnp.dot(u2, wup_ref[c], preferred_element_type=F32)
        hpad_sc[0:HALO, :] = carry_sc[c]
        hpad_sc[HALO:HALO + tm, :] = hcat
        carry_sc[c] = hcat[tm - HALO:tm, :]
        y = cb_ref[c] + cw_ref[c, FFN_CONV - 1:FFN_CONV, :] * hcat
        for kk in range(FFN_CONV - 1):
            y = y + cw_ref[c, kk:kk + 1, :] * hpad_sc[pl.ds(HALO - FFN_CONV + 1 + kk, tm), :]
        act = (_gelu_tanh(y[:, CH_FF:]) * y[:, :CH_FF]).astype(BF16)
        acc = acc + jnp.dot(act, wdn_ref[c], preferred_element_type=F32)
    z = ALPHA * x1 + g2 * acc
    o_ref[...] = _layer_norm(z, g_ref[...], b_ref[...])


def _ffn(x1, mod, wup_c, cw_c, cb_c, wdn_c, ln_g, ln_b):
    B, S, D = x1.shape
    tm = TM_FFN
    nch = wup_c.shape[0]
    row = lambda b, i: (b, i, 0)
    const2 = lambda b, i: (0, 0)
    const3 = lambda b, i: (0, 0, 0)
    vec = pl.BlockSpec((1, D), const2)
    return pl.pallas_call(
        _ffn_kernel,
        grid=(B, S // tm),
        in_specs=[pl.BlockSpec((None, tm, D), row),
                  pl.BlockSpec((None, 6, D), lambda b, i: (b, 0, 0)),
                  _resident(wup_c.shape, const3), _resident(cw_c.shape, const3),
                  _resident(cb_c.shape, const3), _resident(wdn_c.shape, const3),
                  vec, vec],
        out_specs=pl.BlockSpec((None, tm, D), row),
        out_shape=jax.ShapeDtypeStruct((B, S, D), F32),
        scratch_shapes=[pltpu.VMEM((tm + HALO, 2 * CH_FF), F32),
                        pltpu.VMEM((nch, HALO, 2 * CH_FF), F32)],
        compiler_params=_cparams(("parallel", "arbitrary")),
        name="ffn",
    )(x1, mod, wup_c, cw_c, cb_c, wdn_c, ln_g.reshape(1, D), ln_b.reshape(1, D))


def _chunk_value_gate(t, nch):
    lead = t.shape[:-1]
    val = t[..., :D_FF].reshape(*lead, nch, CH_FF)
    gate = t[..., D_FF:].reshape(*lead, nch, CH_FF)
    cat = jnp.concatenate([val, gate], axis=-1)
    return jnp.moveaxis(cat, -2, 0)


def kernel(x, c, w_ada, b_ada, w_in, rel_bias, lru_conv_w, lru_conv_b, lru_wa, lru_ba, lru_wx, lru_bx,
           lru_lambda, w_proj_attn, w_proj_lru, w_out, ln1_g, ln1_b, ffn_w_up, ffn_conv_w, ffn_conv_b,
           ffn_w_down, ln2_g, ln2_b):
    B, S, D = x.shape
    assert w_ada.shape[0] == DEPTH == 1 and D == D_MODEL
    assert S % (DILATED_GROUPS[-1][1] * BLOCK) == 0 and S % TM_IN == 0
    l = 0
    mod = _ada(c, w_ada[l], b_ada[l]).reshape(B, 6, D)

    q, k, v, x_lru, gl, ga, gr = _inproj(x, mod, w_in[l].astype(BF16))

    bias = _bias_tiles(rel_bias, _bucket_maps())
    prev = None
    for g in range(N_GROUPS):
        prev = _attn_group(g, q, k, v, bias, prev)
    o_attn = prev

    wg = _lru_gate_weights(lru_wa[l], lru_wx[l])
    hg = _lru(x_lru, gl, lru_conv_w[l], lru_conv_b[l], wg, lru_ba[l], lru_bx[l], lru_lambda[l])

    x1 = _mix(x, o_attn, hg, ga, gr, mod, w_proj_attn[l].astype(BF16), w_proj_lru[l].astype(BF16),
              w_out[l].astype(BF16), ln1_g[l], ln1_b[l])

    nch = D_FF // CH_FF
    wup_c = _chunk_value_gate(ffn_w_up[l], nch).astype(BF16)
    cw_c = _chunk_value_gate(ffn_conv_w[l], nch)
    cb_c = _chunk_value_gate(ffn_conv_b[l][None, :], nch)
    wdn_c = ffn_w_down[l].reshape(nch, CH_FF, D).astype(BF16)
    return _ffn(x1, mod, wup_c, cw_c, cb_c, wdn_c, ln2_g[l], ln2_b[l])
```

```python
import functools
import math

import jax
import jax.numpy as jnp
from jax import lax
from jax.experimental import pallas as pl
from jax.experimental.pallas import tpu as pltpu

F32 = jnp.float32
BF16 = jnp.bfloat16

D_MODEL = 1024
HEAD_DIM = 64
HEADS_PER_GROUP = 8
DILATED_GROUPS = ((128, 1), (512, 4), (2048, 16))
N_GROUPS = len(DILATED_GROUPS)
GROUP_W = HEADS_PER_GROUP * HEAD_DIM
ATTN_QKV = N_GROUPS * GROUP_W
BLOCK = 128
LOG2E = math.log2(math.e)
NUM_BUCKETS = 32
MAX_EXACT = NUM_BUCKETS // 2
MAX_DISTANCE = 2048
LRU_WIDTH = D_MODEL
LRU_BLOCKS = 16
LRU_BLOCK_W = LRU_WIDTH // LRU_BLOCKS
LRU_CONV = 4
C_RGLRU = 8.0
D_FF = 3 * D_MODEL
FFN_CONV = 3
DEPTH = 1
ALPHA = (2.0 * DEPTH) ** 0.25
LN_EPS = 1e-5

LANES = 128
SUBLANES = 8
VMEM_LIMIT = 56 * 1024 * 1024
TM_IN = 512
TM_MIX = 512
TM_FFN = 512
TS_LRU = 256
CH_IN = GROUP_W
CH_FF = 512
LRU_GRP = 256
HALO = SUBLANES
PAIR_W = 2 * HEAD_DIM
N_PAIRS = GROUP_W // PAIR_W
SLABS = GROUP_W // LANES
ATTN_UNROLL = (5, 2, 8)


def _cparams(sem):
    return pltpu.CompilerParams(dimension_semantics=sem, vmem_limit_bytes=VMEM_LIMIT)


def _resident(shape, index_map):
    return pl.BlockSpec(shape, index_map, pipeline_mode=pl.Buffered(1))


def _gelu_tanh(x):
    return 0.5 * x * (1.0 + jnp.tanh(math.sqrt(2.0 / math.pi) * (x + 0.044715 * (x * x * x))))


def _sigmoid(x):
    return 1.0 / (1.0 + jnp.exp(-x))


def _layer_norm(z, g, b):
    mu = jnp.mean(z, axis=-1, keepdims=True)
    zc = z - mu
    var = jnp.mean(zc * zc, axis=-1, keepdims=True)
    return zc * lax.rsqrt(var + LN_EPS) * g + b


def _ada_kernel(c_ref, w_ref, b_ref, o_ref):
    c = c_ref[...]
    ca = (c * _sigmoid(c)).astype(BF16)
    o_ref[...] = jnp.dot(ca, w_ref[...].astype(BF16), preferred_element_type=F32) + b_ref[...]


def _ada(c, w_ada, b_ada):
    B, D = c.shape
    N = w_ada.shape[1]
    tn = D
    return pl.pallas_call(
        _ada_kernel,
        grid=(N // tn,),
        in_specs=[pl.BlockSpec((B, D), lambda j: (0, 0)),
                  pl.BlockSpec((D, tn), lambda j: (0, j)),
                  pl.BlockSpec((1, tn), lambda j: (0, j))],
        out_specs=pl.BlockSpec((B, tn), lambda j: (0, j)),
        out_shape=jax.ShapeDtypeStruct((B, N), F32),
        compiler_params=_cparams(("arbitrary",)),
        name="ada",
    )(c, w_ada, b_ada.reshape(1, N))


def _inproj_kernel(x_ref, mod_ref, w_ref, g0_ref, g1_ref, g2_ref, xl_ref, gl_ref, ga_ref, gr_ref, split_sc):
    tm = x_ref.shape[0]
    sh1 = mod_ref[0:1, :]
    sc1 = mod_ref[1:2, :]
    u = (x_ref[...] * (1.0 + sc1) + sh1).astype(BF16)

    def mm(c0):
        return jnp.dot(u, w_ref[:, c0:c0 + CH_IN], preferred_element_type=F32)

    def store_group(g, which, res):
        ref = (g0_ref, g1_ref, g2_ref)[g]
        dil = DILATED_GROUPS[g][1]
        if dil == 1:
            ref[which] = res.astype(BF16)
            return
        for s in range(SLABS):
            split_sc[s] = res[:, s * LANES:(s + 1) * LANES]
        rows = tm // dil
        for r in range(dil):
            for s in range(SLABS):
                ref[which, r, :, s * LANES:(s + 1) * LANES] = (
                    split_sc[s, pl.ds(r, rows, stride=dil), :].astype(BF16))

    col = 0
    for which in range(3):
        scale = HEAD_DIM ** -0.5 * LOG2E if which == 0 else None
        for g in range(N_GROUPS):
            res = mm(col)
            if scale is not None:
                res = res * scale
            store_group(g, which, res)
            col += CH_IN
    for j in range(LRU_WIDTH // CH_IN):
        xl_ref[:, j * CH_IN:(j + 1) * CH_IN] = mm(col)
        col += CH_IN
    for j in range(LRU_WIDTH // CH_IN):
        gl_ref[:, j * CH_IN:(j + 1) * CH_IN] = _gelu_tanh(mm(col)).astype(BF16)
        col += CH_IN
    for ref in (ga_ref, gr_ref):
        for j in range(D_MODEL // CH_IN):
            ref[:, j * CH_IN:(j + 1) * CH_IN] = _sigmoid(mm(col)).astype(BF16)
            col += CH_IN


def _inproj(x, mod, w_in_b):
    B, S, D = x.shape
    tm = TM_IN
    ncols = w_in_b.shape[1]
    row = lambda b, i: (b, i, 0)
    bf_shape = jax.ShapeDtypeStruct((B, S, D), BF16)
    row_spec = pl.BlockSpec((None, tm, D), row)
    qkv_shapes, qkv_specs = [], []
    for _, dil in DILATED_GROUPS:
        if dil == 1:
            qkv_shapes.append(jax.ShapeDtypeStruct((3, B, S, GROUP_W), BF16))
            qkv_specs.append(pl.BlockSpec((3, None, tm, GROUP_W), lambda b, i: (0, b, i, 0)))
        else:
            qkv_shapes.append(jax.ShapeDtypeStruct((3, B, dil, S // dil, GROUP_W), BF16))
            qkv_specs.append(pl.BlockSpec((3, None, dil, tm // dil, GROUP_W), lambda b, i: (0, b, 0, i, 0)))
    return pl.pallas_call(
        _inproj_kernel,
        grid=(B, S // tm),
        in_specs=[row_spec,
                  pl.BlockSpec((None, 6, D), lambda b, i: (b, 0, 0)),
                  _resident((D, ncols), lambda b, i: (0, 0))],
        out_specs=qkv_specs + [row_spec, row_spec, row_spec, row_spec],
        out_shape=qkv_shapes + [jax.ShapeDtypeStruct((B, S, LRU_WIDTH), F32), bf_shape, bf_shape, bf_shape],
        scratch_shapes=[pltpu.VMEM((SLABS, tm, LANES), F32)],
        compiler_params=_cparams(("parallel", "parallel")),
        name="inproj",
    )(x, mod, w_in_b)


def _bias_kernel(tab_ref, bucket_ref, o_ref):
    hh = pl.program_id(0)
    bucket = bucket_ref[...]
    acc = jnp.zeros(bucket.shape, F32)
    for b in range(NUM_BUCKETS):
        acc = jnp.where(bucket == b, tab_ref[b, hh], acc)
    qi = lax.broadcasted_iota(jnp.int32, bucket.shape, 0)
    kj = lax.broadcasted_iota(jnp.int32, bucket.shape, 1)
    dist = qi + BLOCK - kj
    band = (dist >= 0) & (dist <= BLOCK)
    o_ref[...] = jnp.where(band, acc * LOG2E, -jnp.inf)


def _bias_tiles(rel_bias, buckets):
    nh = rel_bias.shape[1]
    return pl.pallas_call(
        _bias_kernel,
        grid=(nh,),
        in_specs=[pl.BlockSpec(memory_space=pltpu.SMEM),
                  pl.BlockSpec((None, BLOCK, 2 * BLOCK), lambda h: (h // HEADS_PER_GROUP, 0, 0))],
        out_specs=pl.BlockSpec((None, BLOCK, 2 * BLOCK), lambda h: (h, 0, 0)),
        out_shape=jax.ShapeDtypeStruct((nh, BLOCK, 2 * BLOCK), F32),
        compiler_params=_cparams(("arbitrary",)),
        name="bias_tiles",
    )(rel_bias, buckets)


def _bucket_maps():
    qi = jnp.arange(BLOCK)[:, None]
    kj = jnp.arange(2 * BLOCK)[None, :]
    dist = jnp.maximum(qi + BLOCK - kj, 0)
    maps = []
    for _, dil in DILATED_GROUPS:
        d = dist * dil
        nf = jnp.maximum(d, 1).astype(F32)
        large = MAX_EXACT + (jnp.log(nf / MAX_EXACT) / math.log(MAX_DISTANCE / MAX_EXACT)
                             * (NUM_BUCKETS - MAX_EXACT)).astype(jnp.int32)
        large = jnp.minimum(large, NUM_BUCKETS - 1)
        maps.append(jnp.where(d < MAX_EXACT, d, large))
    return jnp.stack(maps, 0).astype(jnp.int32)


def _attn_kernel(g0_ref, g1_ref, g2_ref, bias_ref, o_ref, part_sc):
    hp = pl.program_id(1)
    lane = lax.broadcasted_iota(jnp.int32, (BLOCK, PAIR_W), 1)
    first_head = lane < HEAD_DIM

    def core(g, qp, kp, vp, first):
        zero = jnp.zeros_like(qp)
        q2 = jnp.concatenate([jnp.where(first_head, qp, zero), jnp.where(first_head, zero, qp)], axis=0)
        bias2 = bias_ref[g * N_PAIRS + hp]
        if first:
            bias2 = bias2[:, BLOCK:]
        logits = lax.dot_general(q2, kp, (((1,), (1,)), ((), ())), preferred_element_type=F32) + bias2
        m = jnp.max(logits, axis=-1, keepdims=True)
        p = jnp.exp2(logits - m)
        s = jnp.sum(p, axis=-1, keepdims=True)
        pv = jnp.dot(p.astype(BF16), vp, preferred_element_type=F32)
        pair = lambda t: jnp.where(first_head, t[:BLOCK], t[BLOCK:])
        return pair(m), pair(s), pair(pv)

    def qkv_block(rows_ref, r0, first):
        qp = rows_ref(0)[pl.ds(r0, BLOCK), :]
        if first:
            return qp, rows_ref(1)[pl.ds(r0, BLOCK), :], rows_ref(2)[pl.ds(r0, BLOCK), :]
        k0 = r0 - BLOCK if isinstance(r0, int) else pl.multiple_of(r0 - BLOCK, BLOCK)
        return qp, rows_ref(1)[pl.ds(k0, 2 * BLOCK), :], rows_ref(2)[pl.ds(k0, 2 * BLOCK), :]

    def dilated_group(g, ref):
        dil = DILATED_GROUPS[g][1]
        nblk = ref.shape[2] // BLOCK

        def body(r, carry):
            for n in range(nblk):
                qp, kp, vp = qkv_block(lambda w: ref.at[w, r], n * BLOCK, n == 0)
                parts = core(g, qp, kp, vp, n == 0)
                for j, t in enumerate(parts):
                    part_sc[3 * (g - 1) + j, pl.ds(n * BLOCK * dil + r, BLOCK, stride=dil), :] = t
            return carry
        lax.fori_loop(0, dil, body, 0, unroll=ATTN_UNROLL[g])

    dilated_group(1, g1_ref)
    dilated_group(2, g2_ref)

    def dense_block(r0, first):
        qp, kp, vp = qkv_block(lambda w: g0_ref.at[w], r0, first)
        m0, s0, pv0 = core(0, qp, kp, vp, first)
        rows = pl.ds(r0, BLOCK)
        m1, s1, pv1 = part_sc[0, rows, :], part_sc[1, rows, :], part_sc[2, rows, :]
        m2, s2, pv2 = part_sc[3, rows, :], part_sc[4, rows, :], part_sc[5, rows, :]
        mx = jnp.maximum(jnp.maximum(m0, m1), m2)
        a0, a1, a2 = jnp.exp2(m0 - mx), jnp.exp2(m1 - mx), jnp.exp2(m2 - mx)
        den = a0 * s0 + a1 * s1 + a2 * s2
        num = a0 * pv0 + a1 * pv1 + a2 * pv2
        o_ref[rows, :] = (num / den).astype(o_ref.dtype)

    dense_block(0, True)

    def body0(n, carry):
        dense_block(pl.multiple_of(n * BLOCK, BLOCK), False)
        return carry
    lax.fori_loop(1, g0_ref.shape[1] // BLOCK, body0, 0, unroll=ATTN_UNROLL[0])


def _attention(qkv0, qkv1, qkv2, bias):
    _, B, S, _ = qkv0.shape
    specs = [pl.BlockSpec((3, None, S, PAIR_W), lambda b, h: (0, b, 0, h))]
    for t in (qkv1, qkv2):
        dil, L = t.shape[2], t.shape[3]
        specs.append(pl.BlockSpec((3, None, dil, L, PAIR_W), lambda b, h: (0, b, 0, 0, h)))
    bias2 = bias.reshape(N_GROUPS * N_PAIRS, 2 * BLOCK, 2 * BLOCK)
    specs.append(_resident(bias2.shape, lambda b, h: (0, 0, 0)))
    return pl.pallas_call(
        _attn_kernel,
        grid=(B, N_PAIRS),
        in_specs=specs,
        out_specs=pl.BlockSpec((None, S, PAIR_W), lambda b, h: (b, 0, h)),
        out_shape=jax.ShapeDtypeStruct((B, S, GROUP_W), BF16),
        scratch_shapes=[pltpu.VMEM((3 * (N_GROUPS - 1), S, LANES), F32)],
        compiler_params=_cparams(("parallel", "parallel")),
        name="attn",
    )(qkv0, qkv1, qkv2, bias2)


def _lru_kernel(xl_ref, gl_ref, cw_ref, cb_ref, wg_ref, ba_ref, bx_ref, lam_ref, o_ref, xpad_sc, hc_sc):
    ts = xl_ref.shape[0]
    W = xl_ref.shape[1]

    @pl.when(pl.program_id(1) == 0)
    def _():
        xpad_sc[0:HALO, :] = jnp.zeros((HALO, W), F32)
        hc_sc[...] = jnp.zeros(hc_sc.shape, F32)

    xpad_sc[HALO:HALO + ts, :] = xl_ref[...]
    xr = cb_ref[...] + cw_ref[0:1, :] * xpad_sc[pl.ds(HALO - LRU_CONV + 1, ts), :]
    for kk in range(1, LRU_CONV):
        xr = xr + cw_ref[kk:kk + 1, :] * xpad_sc[pl.ds(HALO - LRU_CONV + 1 + kk, ts), :]
    xpad_sc[0:HALO, :] = xl_ref[ts - HALO:ts, :]

    xb = xr.astype(BF16)
    ngrp = W // LRU_GRP
    zs = [jnp.dot(xb[:, c * LRU_GRP:(c + 1) * LRU_GRP], wg_ref[c], preferred_element_type=F32)
          for c in range(ngrp)]
    za = jnp.concatenate([z[:, :LRU_GRP] for z in zs], axis=1)
    zx = jnp.concatenate([z[:, LRU_GRP:] for z in zs], axis=1)
    r = _sigmoid(za + ba_ref[...])
    gi = _sigmoid(zx + bx_ref[...])
    nl = -lam_ref[...]
    softplus = jnp.maximum(nl, 0.0) + jnp.log1p(jnp.exp(-jnp.abs(nl)))
    log_a = (-C_RGLRU) * r * softplus
    a = jnp.exp(log_a)
    bt = jnp.sqrt(-jnp.tanh(log_a) * (a * a + 1.0)) * (gi * xr)

    row = lax.broadcasted_iota(jnp.int32, (ts, W), 0)
    sh = 1
    while sh < ts:
        if sh < SUBLANES:
            a_sh = jnp.where(row >= sh, pltpu.roll(a, sh, 0), 1.0)
            b_sh = jnp.where(row >= sh, pltpu.roll(bt, sh, 0), 0.0)
        else:
            a_sh = jnp.concatenate([jnp.ones((sh, W), F32), a[:ts - sh]], axis=0)
            b_sh = jnp.concatenate([jnp.zeros((sh, W), F32), bt[:ts - sh]], axis=0)
        bt = a * b_sh + bt
        a = a * a_sh
        sh *= 2
    h = a * hc_sc[0:1, :] + bt
    hc_sc[...] = jnp.broadcast_to(h[ts - 1:ts, :], hc_sc.shape)
    o_ref[...] = (h * gl_ref[...].astype(F32)).astype(BF16)


def _lru(x_lru, gl, conv_w, conv_b, wg, ba, bx, lam):
    B, S, W = x_lru.shape
    ts = TS_LRU
    row = lambda b, i: (b, i, 0)
    const2 = lambda b, i: (0, 0)
    vec = pl.BlockSpec((1, W), const2)
    return pl.pallas_call(
        _lru_kernel,
        grid=(B, S // ts),
        in_specs=[pl.BlockSpec((None, ts, W), row), pl.BlockSpec((None, ts, W), row),
                  pl.BlockSpec((LRU_CONV, W), const2), vec,
                  pl.BlockSpec(wg.shape, lambda b, i: (0, 0, 0)), vec, vec, vec],
        out_specs=pl.BlockSpec((None, ts, W), row),
        out_shape=jax.ShapeDtypeStruct((B, S, W), BF16),
        scratch_shapes=[pltpu.VMEM((ts + HALO, W), F32), pltpu.VMEM((SUBLANES, W), F32)],
        compiler_params=_cparams(("parallel", "arbitrary")),
        name="lru",
    )(x_lru, gl, conv_w, conv_b.reshape(1, W), wg, ba.reshape(1, W), bx.reshape(1, W), lam.reshape(1, W))


def _lru_gate_weights(wa, wx):
    per = LRU_GRP // LRU_BLOCK_W
    ngrp = LRU_BLOCKS // per

    def bd(w):
        w = w.reshape(ngrp, per, LRU_BLOCK_W, LRU_BLOCK_W)
        eye = jnp.eye(per, dtype=w.dtype)
        t = jnp.einsum('gpcd,pq->gpcqd', w, eye)
        return t.reshape(ngrp, LRU_GRP, LRU_GRP)

    return jnp.concatenate([bd(wa), bd(wx)], axis=2).astype(BF16)


def _mix_kernel(x_ref, oa_ref, hg_ref, ga_ref, gr_ref, mod_ref, wpa_ref, wpl_ref, wo_ref,
                g_ref, b_ref, o_ref):
    ya = jnp.dot(oa_ref[...], wpa_ref[...], preferred_element_type=F32)
    yl = jnp.dot(hg_ref[...], wpl_ref[...], preferred_element_type=F32)
    merged = ga_ref[...].astype(F32) * ya + gr_ref[...].astype(F32) * yl
    mo = jnp.dot(merged.astype(BF16), wo_ref[...], preferred_element_type=F32)
    g1 = mod_ref[2:3, :]
    z = ALPHA * x_ref[...] + g1 * mo
    o_ref[...] = _layer_norm(z, g_ref[...], b_ref[...])


def _mix(x, o_attn, hg, ga, gr, mod, wpa, wpl, wo, ln_g, ln_b):
    B, S, D = x.shape
    tm = TM_MIX
    row = lambda b, i: (b, i, 0)
    const2 = lambda b, i: (0, 0)
    rspec = lambda w: pl.BlockSpec((None, tm, w), row)
    vec = pl.BlockSpec((1, D), const2)
    return pl.pallas_call(
        _mix_kernel,
        grid=(B, S // tm),
        in_specs=[rspec(D), rspec(GROUP_W), rspec(LRU_WIDTH), rspec(D), rspec(D),
                  pl.BlockSpec((None, 6, D), lambda b, i: (b, 0, 0)),
                  _resident(wpa.shape, const2), _resident(wpl.shape, const2), _resident(wo.shape, const2),
                  vec, vec],
        out_specs=rspec(D),
        out_shape=jax.ShapeDtypeStruct((B, S, D), F32),
        compiler_params=_cparams(("parallel", "parallel")),
        name="mix",
    )(x, o_attn, hg, ga, gr, mod, wpa, wpl, wo, ln_g.reshape(1, D), ln_b.reshape(1, D))


def _ffn_kernel(x_ref, mod_ref, wup_ref, cw_ref, cb_ref, wdn_ref, g_ref, b_ref, o_ref, hpad_sc, carry_sc):
    tm = x_ref.shape[0]
    nch = wup_ref.shape[0]

    @pl.when(pl.program_id(1) == 0)
    def _():
        carry_sc[...] = jnp.zeros(carry_sc.shape, F32)

    x1 = x_ref[...]
    sh2 = mod_ref[3:4, :]
    sc2 = mod_ref[4:5, :]
    g2 = mod_ref[5:6, :]
    u2 = (x1 * (1.0 + sc2) + sh2).astype(BF16)
    acc = jnp.zeros((tm, D_MODEL), F32)
    for c in range(nch):
        hcat = jnp.dot(u2, wup_ref[c], preferred_element_type=F32)
        hpad_sc[0:HALO, :] = carry_sc[c]
        hpad_sc[HALO:HALO + tm, :] = hcat
        carry_sc[c] = hcat[tm - HALO:tm, :]
        y = cb_ref[c] + cw_ref[c, FFN_CONV - 1:FFN_CONV, :] * hcat
        for kk in range(FFN_CONV - 1):
            y = y + cw_ref[c, kk:kk + 1, :] * hpad_sc[pl.ds(HALO - FFN_CONV + 1 + kk, tm), :]
        act = (_gelu_tanh(y[:, CH_FF:]) * y[:, :CH_FF]).astype(BF16)
        acc = acc + jnp.dot(act, wdn_ref[c], preferred_element_type=F32)
    z = ALPHA * x1 + g2 * acc
    o_ref[...] = _layer_norm(z, g_ref[...], b_ref[...])


def _ffn(x1, mod, wup_c, cw_c, cb_c, wdn_c, ln_g, ln_b):
    B, S, D = x1.shape
    tm = TM_FFN
    nch = wup_c.shape[0]
    row = lambda b, i: (b, i, 0)
    const2 = lambda b, i: (0, 0)
    const3 = lambda b, i: (0, 0, 0)
    vec = pl.BlockSpec((1, D), const2)
    return pl.pallas_call(
        _ffn_kernel,
        grid=(B, S // tm),
        in_specs=[pl.BlockSpec((None, tm, D), row),
                  pl.BlockSpec((None, 6, D), lambda b, i: (b, 0, 0)),
                  _resident(wup_c.shape, const3), _resident(cw_c.shape, const3),
                  _resident(cb_c.shape, const3), _resident(wdn_c.shape, const3),
                  vec, vec],
        out_specs=pl.BlockSpec((None, tm, D), row),
        out_shape=jax.ShapeDtypeStruct((B, S, D), F32),
        scratch_shapes=[pltpu.VMEM((tm + HALO, 2 * CH_FF), F32),
                        pltpu.VMEM((nch, HALO, 2 * CH_FF), F32)],
        compiler_params=_cparams(("parallel", "arbitrary")),
        name="ffn",
    )(x1, mod, wup_c, cw_c, cb_c, wdn_c, ln_g.reshape(1, D), ln_b.reshape(1, D))


def _chunk_value_gate(t, nch):
    lead = t.shape[:-1]
    val = t[..., :D_FF].reshape(*lead, nch, CH_FF)
    gate = t[..., D_FF:].reshape(*lead, nch, CH_FF)
    cat = jnp.concatenate([val, gate], axis=-1)
    return jnp.moveaxis(cat, -2, 0)


def kernel(x, c, w_ada, b_ada, w_in, rel_bias, lru_conv_w, lru_conv_b, lru_wa, lru_ba, lru_wx, lru_bx,
           lru_lambda, w_proj_attn, w_proj_lru, w_out, ln1_g, ln1_b, ffn_w_up, ffn_conv_w, ffn_conv_b,
           ffn_w_down, ln2_g, ln2_b):
    B, S, D = x.shape
    assert w_ada.shape[0] == DEPTH == 1 and D == D_MODEL
    assert S == DILATED_GROUPS[-1][1] * BLOCK and S % TM_IN == 0
    assert all(window // dil == BLOCK for window, dil in DILATED_GROUPS)
    l = 0
    mod = _ada(c, w_ada[l], b_ada[l]).reshape(B, 6, D)

    qkv0, qkv1, qkv2, x_lru, gl, ga, gr = _inproj(x, mod, w_in[l].astype(BF16))

    bias = _bias_tiles(rel_bias, _bucket_maps())
    o_attn = _attention(qkv0, qkv1, qkv2, bias)

    wg = _lru_gate_weights(lru_wa[l], lru_wx[l])
    hg = _lru(x_lru, gl, lru_conv_w[l], lru_conv_b[l], wg, lru_ba[l], lru_bx[l], lru_lambda[l])

    x1 = _mix(x, o_attn, hg, ga, gr, mod, w_proj_attn[l].astype(BF16), w_proj_lru[l].astype(BF16),
              w_out[l].astype(BF16), ln1_g[l], ln1_b[l])

    nch = D_FF // CH_FF
    wup_c = _chunk_value_gate(ffn_w_up[l], nch).astype(BF16)
    cw_c = _chunk_value_gate(ffn_conv_w[l], nch)
    cb_c = _chunk_value_gate(ffn_conv_b[l][None, :], nch)
    wdn_c = ffn_w_down[l].reshape(nch, CH_FF, D).astype(BF16)
    return _ffn(x1, mod, wup_c, cw_c, cb_c, wdn_c, ln2_g[l], ln2_b[l])
```

```python
import functools
import math

import jax
import jax.numpy as jnp
from jax import lax
from jax.experimental import pallas as pl
from jax.experimental.pallas import tpu as pltpu

F32 = jnp.float32
BF16 = jnp.bfloat16

D_MODEL = 1024
HEAD_DIM = 64
HEADS_PER_GROUP = 8
DILATED_GROUPS = ((128, 1), (512, 4), (2048, 16))
N_GROUPS = len(DILATED_GROUPS)
GROUP_W = HEADS_PER_GROUP * HEAD_DIM
ATTN_QKV = N_GROUPS * GROUP_W
BLOCK = 128
LOG2E = math.log2(math.e)
NUM_BUCKETS = 32
MAX_EXACT = NUM_BUCKETS // 2
MAX_DISTANCE = 2048
LRU_WIDTH = D_MODEL
LRU_BLOCKS = 16
LRU_BLOCK_W = LRU_WIDTH // LRU_BLOCKS
LRU_CONV = 4
C_RGLRU = 8.0
D_FF = 3 * D_MODEL
FFN_CONV = 3
DEPTH = 1
ALPHA = (2.0 * DEPTH) ** 0.25
LN_EPS = 1e-5

LANES = 128
SUBLANES = 8
VMEM_LIMIT = 56 * 1024 * 1024
TM_IN = 512
TM_MIX = 512
TM_FFN = 512
TS_LRU = 512
LRU_SCAN_UNROLL = 4
CH_FF = 1536
LRU_GRP = 256
HALO = SUBLANES
PAIR_W = 2 * HEAD_DIM
N_PAIRS = GROUP_W // PAIR_W
SLABS = GROUP_W // LANES
ATTN_UNROLL = (5, 2, 8)


def _cparams(sem):
    return pltpu.CompilerParams(dimension_semantics=sem, vmem_limit_bytes=VMEM_LIMIT)


def _resident(shape, index_map):
    return pl.BlockSpec(shape, index_map, pipeline_mode=pl.Buffered(1))


def _gelu_tanh(x):
    return 0.5 * x * (1.0 + jnp.tanh(math.sqrt(2.0 / math.pi) * (x + 0.044715 * (x * x * x))))


def _sigmoid(x):
    return 0.5 * jnp.tanh(0.5 * x) + 0.5


def _layer_norm(z, g, b):
    mu = jnp.mean(z, axis=-1, keepdims=True)
    zc = z - mu
    var = jnp.mean(zc * zc, axis=-1, keepdims=True)
    return zc * lax.rsqrt(var + LN_EPS) * g + b


def _ada_kernel(c_ref, w_ref, b_ref, o_ref):
    c = c_ref[...]
    ca = (c * _sigmoid(c)).astype(BF16)
    o_ref[...] = jnp.dot(ca, w_ref[...].astype(BF16), preferred_element_type=F32) + b_ref[...]


def _ada(c, w_ada, b_ada):
    B, D = c.shape
    N = w_ada.shape[1]
    tn = D
    return pl.pallas_call(
        _ada_kernel,
        grid=(N // tn,),
        in_specs=[pl.BlockSpec((B, D), lambda j: (0, 0)),
                  pl.BlockSpec((D, tn), lambda j: (0, j)),
                  pl.BlockSpec((1, tn), lambda j: (0, j))],
        out_specs=pl.BlockSpec((B, tn), lambda j: (0, j)),
        out_shape=jax.ShapeDtypeStruct((B, N), F32),
        compiler_params=_cparams(("arbitrary",)),
        name="ada",
    )(c, w_ada, b_ada.reshape(1, N))


def _inproj_kernel(x_ref, mod_ref, w_ref, g0_ref, g1_ref, g2_ref, xl_ref, gl_ref, ga_ref, gr_ref, split_sc):
    tm = x_ref.shape[0]
    sh1 = mod_ref[0:1, :]
    sc1 = mod_ref[1:2, :]
    u = (x_ref[...] * (1.0 + sc1) + sh1).astype(BF16)

    def mm(c0, n):
        return jnp.dot(u, w_ref[:, c0:c0 + n], preferred_element_type=F32)

    n_split = [0]

    def store_group(g, which, res):
        ref = (g0_ref, g1_ref, g2_ref)[g]
        dil = DILATED_GROUPS[g][1]
        if dil == 1:
            ref[which] = res.astype(BF16)
            return
        buf = n_split[0] % split_sc.shape[0]
        n_split[0] += 1
        for s in range(SLABS):
            split_sc[buf, s] = res[:, s * LANES:(s + 1) * LANES]
        rows = tm // dil
        for r in range(dil):
            for s in range(SLABS):
                ref[which, r, :, s * LANES:(s + 1) * LANES] = (
                    split_sc[buf, s, pl.ds(r, rows, stride=dil), :].astype(BF16))

    for which in range(3):
        res = mm(which * ATTN_QKV, ATTN_QKV)
        if which == 0:
            res = res * (HEAD_DIM ** -0.5 * LOG2E)
        for g in range(N_GROUPS):
            store_group(g, which, res[:, g * GROUP_W:(g + 1) * GROUP_W])
    col = 3 * ATTN_QKV
    xl_ref[...] = mm(col, LRU_WIDTH)
    col += LRU_WIDTH
    gl_ref[...] = _gelu_tanh(mm(col, LRU_WIDTH)).astype(BF16)
    col += LRU_WIDTH
    for ref in (ga_ref, gr_ref):
        ref[...] = _sigmoid(mm(col, D_MODEL)).astype(BF16)
        col += D_MODEL


def _inproj(x, mod, w_in_b):
    B, S, D = x.shape
    tm = TM_IN
    ncols = w_in_b.shape[1]
    row = lambda b, i: (b, i, 0)
    bf_shape = jax.ShapeDtypeStruct((B, S, D), BF16)
    row_spec = pl.BlockSpec((None, tm, D), row)
    qkv_shapes, qkv_specs = [], []
    for _, dil in DILATED_GROUPS:
        if dil == 1:
            qkv_shapes.append(jax.ShapeDtypeStruct((3, B, S, GROUP_W), BF16))
            qkv_specs.append(pl.BlockSpec((3, None, tm, GROUP_W), lambda b, i: (0, b, i, 0)))
        else:
            qkv_shapes.append(jax.ShapeDtypeStruct((3, B, dil, S // dil, GROUP_W), BF16))
            qkv_specs.append(pl.BlockSpec((3, None, dil, tm // dil, GROUP_W), lambda b, i: (0, b, 0, i, 0)))
    return pl.pallas_call(
        _inproj_kernel,
        grid=(B, S // tm),
        in_specs=[row_spec,
                  pl.BlockSpec((None, 6, D), lambda b, i: (b, 0, 0)),
                  _resident((D, ncols), lambda b, i: (0, 0))],
        out_specs=qkv_specs + [row_spec, row_spec, row_spec, row_spec],
        out_shape=qkv_shapes + [jax.ShapeDtypeStruct((B, S, LRU_WIDTH), F32), bf_shape, bf_shape, bf_shape],
        scratch_shapes=[pltpu.VMEM((2, SLABS, tm, LANES), F32)],
        compiler_params=_cparams(("parallel", "parallel")),
        name="inproj",
    )(x, mod, w_in_b)


def _bias_kernel(tab_ref, bucket_ref, o_ref):
    hh = pl.program_id(0)
    bucket = bucket_ref[...]
    acc = jnp.zeros(bucket.shape, F32)
    for b in range(NUM_BUCKETS):
        acc = jnp.where(bucket == b, tab_ref[b, hh], acc)
    qi = lax.broadcasted_iota(jnp.int32, bucket.shape, 0)
    kj = lax.broadcasted_iota(jnp.int32, bucket.shape, 1)
    dist = qi + BLOCK - kj
    band = (dist >= 0) & (dist <= BLOCK)
    o_ref[...] = jnp.where(band, acc * LOG2E, -jnp.inf)


def _bias_tiles(rel_bias, buckets):
    nh = rel_bias.shape[1]
    return pl.pallas_call(
        _bias_kernel,
        grid=(nh,),
        in_specs=[pl.BlockSpec(memory_space=pltpu.SMEM),
                  pl.BlockSpec((None, BLOCK, 2 * BLOCK), lambda h: (h // HEADS_PER_GROUP, 0, 0))],
        out_specs=pl.BlockSpec((None, BLOCK, 2 * BLOCK), lambda h: (h, 0, 0)),
        out_shape=jax.ShapeDtypeStruct((nh, BLOCK, 2 * BLOCK), F32),
        compiler_params=_cparams(("arbitrary",)),
        name="bias_tiles",
    )(rel_bias, buckets)


def _bucket_maps():
    qi = jnp.arange(BLOCK)[:, None]
    kj = jnp.arange(2 * BLOCK)[None, :]
    dist = jnp.maximum(qi + BLOCK - kj, 0)
    maps = []
    for _, dil in DILATED_GROUPS:
        d = dist * dil
        nf = jnp.maximum(d, 1).astype(F32)
        large = MAX_EXACT + (jnp.log(nf / MAX_EXACT) / math.log(MAX_DISTANCE / MAX_EXACT)
                             * (NUM_BUCKETS - MAX_EXACT)).astype(jnp.int32)
        large = jnp.minimum(large, NUM_BUCKETS - 1)
        maps.append(jnp.where(d < MAX_EXACT, d, large))
    return jnp.stack(maps, 0).astype(jnp.int32)


def _attn_kernel(g0_ref, g1_ref, g2_ref, bias_ref, o_ref, part_sc):
    hp = pl.program_id(1)
    lane = lax.broadcasted_iota(jnp.int32, (BLOCK, PAIR_W), 1)
    first_head = lane < HEAD_DIM

    def core(g, qp, kp, vp, first):
        zero = jnp.zeros_like(qp)
        q2 = jnp.concatenate([jnp.where(first_head, qp, zero), jnp.where(first_head, zero, qp)], axis=0)
        bias2 = bias_ref[g * N_PAIRS + hp]
        if first:
            bias2 = bias2[:, BLOCK:]
        logits = lax.dot_general(q2, kp, (((1,), (1,)), ((), ())), preferred_element_type=F32) + bias2
        m = jnp.max(logits, axis=-1, keepdims=True)
        p = jnp.exp2(logits - m)
        s = jnp.sum(p, axis=-1, keepdims=True)
        pv = jnp.dot(p.astype(BF16), vp, preferred_element_type=F32)
        pair = lambda t: jnp.where(first_head, t[:BLOCK], t[BLOCK:])
        return pair(m), pair(s), pair(pv)

    def qkv_block(rows_ref, r0, first):
        qp = rows_ref(0)[pl.ds(r0, BLOCK), :]
        if first:
            return qp, rows_ref(1)[pl.ds(r0, BLOCK), :], rows_ref(2)[pl.ds(r0, BLOCK), :]
        k0 = r0 - BLOCK if isinstance(r0, int) else pl.multiple_of(r0 - BLOCK, BLOCK)
        return qp, rows_ref(1)[pl.ds(k0, 2 * BLOCK), :], rows_ref(2)[pl.ds(k0, 2 * BLOCK), :]

    def dilated_group(g, ref):
        dil = DILATED_GROUPS[g][1]
        nblk = ref.shape[2] // BLOCK

        def body(r, carry):
            for n in range(nblk):
                qp, kp, vp = qkv_block(lambda w: ref.at[w, r], n * BLOCK, n == 0)
                parts = core(g, qp, kp, vp, n == 0)
                for j, t in enumerate(parts):
                    part_sc[3 * (g - 1) + j, pl.ds(n * BLOCK * dil + r, BLOCK, stride=dil), :] = t
            return carry
        lax.fori_loop(0, dil, body, 0, unroll=ATTN_UNROLL[g])

    dilated_group(1, g1_ref)
    dilated_group(2, g2_ref)

    def dense_block(r0, first):
        qp, kp, vp = qkv_block(lambda w: g0_ref.at[w], r0, first)
        m0, s0, pv0 = core(0, qp, kp, vp, first)
        rows = pl.ds(r0, BLOCK)
        m1, s1, pv1 = part_sc[0, rows, :], part_sc[1, rows, :], part_sc[2, rows, :]
        m2, s2, pv2 = part_sc[3, rows, :], part_sc[4, rows, :], part_sc[5, rows, :]
        mx = jnp.maximum(jnp.maximum(m0, m1), m2)
        a0, a1, a2 = jnp.exp2(m0 - mx), jnp.exp2(m1 - mx), jnp.exp2(m2 - mx)
        den = a0 * s0 + a1 * s1 + a2 * s2
        num = a0 * pv0 + a1 * pv1 + a2 * pv2
        o_ref[rows, :] = (num / den).astype(o_ref.dtype)

    dense_block(0, True)

    def body0(n, carry):
        dense_block(pl.multiple_of(n * BLOCK, BLOCK), False)
        return carry
    lax.fori_loop(1, g0_ref.shape[1] // BLOCK, body0, 0, unroll=ATTN_UNROLL[0])


def _attention(qkv0, qkv1, qkv2, bias):
    _, B, S, _ = qkv0.shape
    specs = [pl.BlockSpec((3, None, S, PAIR_W), lambda b, h: (0, b, 0, h))]
    for t in (qkv1, qkv2):
        dil, L = t.shape[2], t.shape[3]
        specs.append(pl.BlockSpec((3, None, dil, L, PAIR_W), lambda b, h: (0, b, 0, 0, h)))
    bias2 = bias.reshape(N_GROUPS * N_PAIRS, 2 * BLOCK, 2 * BLOCK)
    specs.append(_resident(bias2.shape, lambda b, h: (0, 0, 0)))
    return pl.pallas_call(
        _attn_kernel,
        grid=(B, N_PAIRS),
        in_specs=specs,
        out_specs=pl.BlockSpec((None, S, PAIR_W), lambda b, h: (b, 0, h)),
        out_shape=jax.ShapeDtypeStruct((B, S, GROUP_W), BF16),
        scratch_shapes=[pltpu.VMEM((3 * (N_GROUPS - 1), S, LANES), F32)],
        compiler_params=_cparams(("parallel", "parallel")),
        name="attn",
    )(qkv0, qkv1, qkv2, bias2)


def _lru_kernel(xl_ref, gl_ref, cw_ref, cb_ref, wg_ref, ba_ref, bx_ref, lam_ref, o_ref,
                xpad_sc, carry_sc, ta_sc, tb_sc, th_sc, tp_sc):
    ts, W = xl_ref.shape
    nseg = SUBLANES
    seg = ts // nseg
    nsl = W // LANES

    @pl.when(pl.program_id(1) == 0)
    def _():
        xpad_sc[0:HALO, :] = jnp.zeros((HALO, W), F32)
        carry_sc[...] = jnp.zeros(carry_sc.shape, F32)

    xpad_sc[HALO:HALO + ts, :] = xl_ref[...]
    xp = xpad_sc[...]
    t = cw_ref[0:1, :] * xp
    for kk in range(1, LRU_CONV):
        t = cw_ref[kk:kk + 1, :] * xp + pltpu.roll(t, 1, 0)
    xr = t[HALO:, :] + cb_ref[...]
    xpad_sc[0:HALO, :] = xl_ref[ts - HALO:ts, :]

    xb = xr.astype(BF16)
    ngrp = W // LRU_GRP
    zs = [jnp.dot(xb[:, c * LRU_GRP:(c + 1) * LRU_GRP], wg_ref[c], preferred_element_type=F32)
          for c in range(ngrp)]
    za = jnp.concatenate([z[:, :LRU_GRP] for z in zs], axis=1)
    zx = jnp.concatenate([z[:, LRU_GRP:] for z in zs], axis=1)
    r = _sigmoid(za + ba_ref[...])
    gi = _sigmoid(zx + bx_ref[...])
    nl = -lam_ref[...]
    softplus = jnp.maximum(nl, 0.0) + jnp.log1p(jnp.exp(-jnp.abs(nl)))
    log_a = r * ((-C_RGLRU) * softplus)
    a = jnp.exp(log_a)
    bt = jnp.sqrt(-jnp.tanh(log_a) * (a * a + 1.0)) * (gi * xr)

    for c in range(nsl):
        for i in range(nseg):
            dst = pl.ds(i, seg, stride=nseg)
            ta_sc[c, dst, :] = a[i * seg:(i + 1) * seg, c * LANES:(c + 1) * LANES]
            tb_sc[c, dst, :] = bt[i * seg:(i + 1) * seg, c * LANES:(c + 1) * LANES]

    def step(j, hp):
        h, p = hp
        rows = pl.ds(pl.multiple_of(j * nseg, nseg), nseg)
        aj = ta_sc[:, rows, :]
        h = aj * h + tb_sc[:, rows, :]
        p = aj * p
        th_sc[:, rows, :] = h
        tp_sc[:, rows, :] = p
        return h, p

    state = (jnp.zeros((nsl, nseg, LANES), F32), jnp.ones((nsl, nseg, LANES), F32))
    x_end, p_end = lax.fori_loop(0, seg, step, state, unroll=LRU_SCAN_UNROLL)

    sub = lax.broadcasted_iota(jnp.int32, (nsl, nseg, LANES), 1)
    sh = 1
    while sh < nseg:
        x_sh = jnp.where(sub >= sh, pltpu.roll(x_end, sh, 1), 0.0)
        p_sh = jnp.where(sub >= sh, pltpu.roll(p_end, sh, 1), 1.0)
        x_end = p_end * x_sh + x_end
        p_end = p_end * p_sh
        sh *= 2
    carry = carry_sc[...]
    leave = x_end + p_end * carry
    h_init = jnp.where(sub >= 1, pltpu.roll(leave, 1, 1), carry)
    carry_sc[...] = jnp.broadcast_to(leave[:, nseg - 1:nseg, :], carry_sc.shape)

    for c in range(nsl):
        for i in range(nseg):
            src = pl.ds(i, seg, stride=nseg)
            h = th_sc[c, src, :] + tp_sc[c, src, :] * h_init[c, i:i + 1, :]
            gate = gl_ref[i * seg:(i + 1) * seg, c * LANES:(c + 1) * LANES].astype(F32)
            o_ref[i * seg:(i + 1) * seg, c * LANES:(c + 1) * LANES] = (h * gate).astype(BF16)


def _lru(x_lru, gl, conv_w, conv_b, wg, ba, bx, lam):
    B, S, W = x_lru.shape
    ts = TS_LRU
    row = lambda b, i: (b, i, 0)
    const2 = lambda b, i: (0, 0)
    vec = pl.BlockSpec((1, W), const2)
    slabs = pltpu.VMEM((W // LANES, ts, LANES), F32)
    return pl.pallas_call(
        _lru_kernel,
        grid=(B, S // ts),
        in_specs=[pl.BlockSpec((None, ts, W), row), pl.BlockSpec((None, ts, W), row),
                  pl.BlockSpec((LRU_CONV, W), const2), vec,
                  pl.BlockSpec(wg.shape, lambda b, i: (0, 0, 0)), vec, vec, vec],
        out_specs=pl.BlockSpec((None, ts, W), row),
        out_shape=jax.ShapeDtypeStruct((B, S, W), BF16),
        scratch_shapes=[pltpu.VMEM((ts + HALO, W), F32), pltpu.VMEM((W // LANES, SUBLANES, LANES), F32),
                        slabs, slabs, slabs, slabs],
        compiler_params=_cparams(("parallel", "arbitrary")),
        name="lru",
    )(x_lru, gl, conv_w, conv_b.reshape(1, W), wg, ba.reshape(1, W), bx.reshape(1, W), lam.reshape(1, W))


def _lru_gate_weights(wa, wx):
    per = LRU_GRP // LRU_BLOCK_W
    ngrp = LRU_BLOCKS // per

    def bd(w):
        w = w.reshape(ngrp, per, LRU_BLOCK_W, LRU_BLOCK_W)
        eye = jnp.eye(per, dtype=w.dtype)
        t = jnp.einsum('gpcd,pq->gpcqd', w, eye)
        return t.reshape(ngrp, LRU_GRP, LRU_GRP)

    return jnp.concatenate([bd(wa), bd(wx)], axis=2).astype(BF16)


def _mix_kernel(x_ref, oa_ref, hg_ref, ga_ref, gr_ref, mod_ref, wpa_ref, wpl_ref, wo_ref,
                g_ref, b_ref, o_ref):
    ya = jnp.dot(oa_ref[...], wpa_ref[...], preferred_element_type=F32)
    yl = jnp.dot(hg_ref[...], wpl_ref[...], preferred_element_type=F32)
    merged = ga_ref[...].astype(F32) * ya + gr_ref[...].astype(F32) * yl
    mo = jnp.dot(merged.astype(BF16), wo_ref[...], preferred_element_type=F32)
    g1 = mod_ref[2:3, :]
    z = ALPHA * x_ref[...] + g1 * mo
    o_ref[...] = _layer_norm(z, g_ref[...], b_ref[...])


def _mix(x, o_attn, hg, ga, gr, mod, wpa, wpl, wo, ln_g, ln_b):
    B, S, D = x.shape
    tm = TM_MIX
    row = lambda b, i: (b, i, 0)
    const2 = lambda b, i: (0, 0)
    rspec = lambda w: pl.BlockSpec((None, tm, w), row)
    vec = pl.BlockSpec((1, D), const2)
    return pl.pallas_call(
        _mix_kernel,
        grid=(B, S // tm),
        in_specs=[rspec(D), rspec(GROUP_W), rspec(LRU_WIDTH), rspec(D), rspec(D),
                  pl.BlockSpec((None, 6, D), lambda b, i: (b, 0, 0)),
                  _resident(wpa.shape, const2), _resident(wpl.shape, const2), _resident(wo.shape, const2),
                  vec, vec],
        out_specs=rspec(D),
        out_shape=jax.ShapeDtypeStruct((B, S, D), F32),
        compiler_params=_cparams(("parallel", "parallel")),
        name="mix",
    )(x, o_attn, hg, ga, gr, mod, wpa, wpl, wo, ln_g.reshape(1, D), ln_b.reshape(1, D))


def _ffn_kernel(x_ref, mod_ref, wup_ref, cw_ref, cb_ref, wdn_ref, g_ref, b_ref, o_ref, carry_sc):
    tm = x_ref.shape[0]
    nch = wup_ref.shape[0]

    @pl.when(pl.program_id(1) == 0)
    def _():
        carry_sc[...] = jnp.zeros(carry_sc.shape, F32)

    x1 = x_ref[...]
    sh2 = mod_ref[3:4, :]
    sc2 = mod_ref[4:5, :]
    g2 = mod_ref[5:6, :]
    u2 = (x1 * (1.0 + sc2) + sh2).astype(BF16)
    acc = jnp.zeros((tm, D_MODEL), F32)
    for c in range(nch):
        hcat = jnp.dot(u2, wup_ref[c], preferred_element_type=F32)
        hx = jnp.concatenate([carry_sc[c], hcat], axis=0)
        carry_sc[c] = hcat[tm - HALO:tm, :]
        t = cw_ref[c, 0:1, :] * hx
        for kk in range(1, FFN_CONV):
            t = cw_ref[c, kk:kk + 1, :] * hx + pltpu.roll(t, 1, 0)
        y = t[HALO:, :] + cb_ref[c]
        act = (_gelu_tanh(y[:, CH_FF:]) * y[:, :CH_FF]).astype(BF16)
        acc = acc + jnp.dot(act, wdn_ref[c], preferred_element_type=F32)
    z = ALPHA * x1 + g2 * acc
    o_ref[...] = _layer_norm(z, g_ref[...], b_ref[...])


def _ffn(x1, mod, wup_c, cw_c, cb_c, wdn_c, ln_g, ln_b):
    B, S, D = x1.shape
    tm = TM_FFN
    nch = wup_c.shape[0]
    row = lambda b, i: (b, i, 0)
    const2 = lambda b, i: (0, 0)
    const3 = lambda b, i: (0, 0, 0)
    vec = pl.BlockSpec((1, D), const2)
    return pl.pallas_call(
        _ffn_kernel,
        grid=(B, S // tm),
        in_specs=[pl.BlockSpec((None, tm, D), row),
                  pl.BlockSpec((None, 6, D), lambda b, i: (b, 0, 0)),
                  _resident(wup_c.shape, const3), _resident(cw_c.shape, const3),
                  _resident(cb_c.shape, const3), _resident(wdn_c.shape, const3),
                  vec, vec],
        out_specs=pl.BlockSpec((None, tm, D), row),
        out_shape=jax.ShapeDtypeStruct((B, S, D), F32),
        scratch_shapes=[pltpu.VMEM((nch, HALO, 2 * CH_FF), F32)],
        compiler_params=_cparams(("parallel", "arbitrary")),
        name="ffn",
    )(x1, mod, wup_c, cw_c, cb_c, wdn_c, ln_g.reshape(1, D), ln_b.reshape(1, D))


def _chunk_value_gate(t, nch):
    lead = t.shape[:-1]
    val = t[..., :D_FF].reshape(*lead, nch, CH_FF)
    gate = t[..., D_FF:].reshape(*lead, nch, CH_FF)
    cat = jnp.concatenate([val, gate], axis=-1)
    return jnp.moveaxis(cat, -2, 0)


def kernel(x, c, w_ada, b_ada, w_in, rel_bias, lru_conv_w, lru_conv_b, lru_wa, lru_ba, lru_wx, lru_bx,
           lru_lambda, w_proj_attn, w_proj_lru, w_out, ln1_g, ln1_b, ffn_w_up, ffn_conv_w, ffn_conv_b,
           ffn_w_down, ln2_g, ln2_b):
    B, S, D = x.shape
    assert w_ada.shape[0] == DEPTH == 1 and D == D_MODEL
    assert S == DILATED_GROUPS[-1][1] * BLOCK and S % TM_IN == 0
    assert all(window // dil == BLOCK for window, dil in DILATED_GROUPS)
    l = 0
    mod = _ada(c, w_ada[l], b_ada[l]).reshape(B, 6, D)

    qkv0, qkv1, qkv2, x_lru, gl, ga, gr = _inproj(x, mod, w_in[l].astype(BF16))

    bias = _bias_tiles(rel_bias, _bucket_maps())
    o_attn = _attention(qkv0, qkv1, qkv2, bias)

    wg = _lru_gate_weights(lru_wa[l], lru_wx[l])
    hg = _lru(x_lru, gl, lru_conv_w[l], lru_conv_b[l], wg, lru_ba[l], lru_bx[l], lru_lambda[l])

    x1 = _mix(x, o_attn, hg, ga, gr, mod, w_proj_attn[l].astype(BF16), w_proj_lru[l].astype(BF16),
              w_out[l].astype(BF16), ln1_g[l], ln1_b[l])

    nch = D_FF // CH_FF
    wup_c = _chunk_value_gate(ffn_w_up[l], nch).astype(BF16)
    cw_c = _chunk_value_gate(ffn_conv_w[l], nch)
    cb_c = _chunk_value_gate(ffn_conv_b[l][None, :], nch)
    wdn_c = ffn_w_down[l].reshape(nch, CH_FF, D).astype(BF16)
    return _ffn(x1, mod, wup_c, cw_c, cb_c, wdn_c, ln2_g[l], ln2_b[l])
```

```python
import functools
import math

import jax
import jax.numpy as jnp
from jax import lax
from jax.experimental import pallas as pl
from jax.experimental.pallas import tpu as pltpu

F32 = jnp.float32
BF16 = jnp.bfloat16

D_MODEL = 1024
HEAD_DIM = 64
HEADS_PER_GROUP = 8
DILATED_GROUPS = ((128, 1), (512, 4), (2048, 16))
N_GROUPS = len(DILATED_GROUPS)
GROUP_W = HEADS_PER_GROUP * HEAD_DIM
ATTN_QKV = N_GROUPS * GROUP_W
BLOCK = 128
LOG2E = math.log2(math.e)
NUM_BUCKETS = 32
MAX_EXACT = NUM_BUCKETS // 2
MAX_DISTANCE = 2048
LRU_WIDTH = D_MODEL
LRU_BLOCKS = 16
LRU_BLOCK_W = LRU_WIDTH // LRU_BLOCKS
LRU_CONV = 4
C_RGLRU = 8.0
D_FF = 3 * D_MODEL
FFN_CONV = 3
DEPTH = 1
ALPHA = (2.0 * DEPTH) ** 0.25
LN_EPS = 1e-5

LANES = 128
SUBLANES = 8
VMEM_LIMIT = 56 * 1024 * 1024
TM_IN = 512
TM_MIX = 512
MIX_SPLIT = 2
TM_FFN = 512
TS_LRU = 512
LRU_SCAN_UNROLL = 4
FF_CHUNKS = (1536, 1536)
assert sum(FF_CHUNKS) == D_FF
LRU_GRP = 256
HALO = SUBLANES
PAIR_W = 2 * HEAD_DIM
N_PAIRS = GROUP_W // PAIR_W
SLABS = GROUP_W // LANES
ATTN_UNROLL = (15, 4, 16)


def _cparams(sem):
    return pltpu.CompilerParams(dimension_semantics=sem, vmem_limit_bytes=VMEM_LIMIT)


def _resident(shape, index_map):
    return pl.BlockSpec(shape, index_map, pipeline_mode=pl.Buffered(1))


def _gelu_tanh(x):
    return 0.5 * x * (1.0 + jnp.tanh(math.sqrt(2.0 / math.pi) * (x + 0.044715 * (x * x * x))))


def _sigmoid(x):
    return 0.5 * jnp.tanh(0.5 * x) + 0.5


def _layer_norm(z, g, b):
    mu = jnp.mean(z, axis=-1, keepdims=True)
    zc = z - mu
    var = jnp.mean(zc * zc, axis=-1, keepdims=True)
    return zc * lax.rsqrt(var + LN_EPS) * g + b


def _ada_kernel(c_ref, w_ref, b_ref, o_ref):
    c = c_ref[...]
    ca = (c * _sigmoid(c)).astype(BF16)
    o_ref[...] = jnp.dot(ca, w_ref[...].astype(BF16), preferred_element_type=F32) + b_ref[...]


def _ada(c, w_ada, b_ada):
    B, D = c.shape
    N = w_ada.shape[1]
    tn = D
    return pl.pallas_call(
        _ada_kernel,
        grid=(N // tn,),
        in_specs=[pl.BlockSpec((B, D), lambda j: (0, 0)),
                  pl.BlockSpec((D, tn), lambda j: (0, j)),
                  pl.BlockSpec((1, tn), lambda j: (0, j))],
        out_specs=pl.BlockSpec((B, tn), lambda j: (0, j)),
        out_shape=jax.ShapeDtypeStruct((B, N), F32),
        compiler_params=_cparams(("arbitrary",)),
        name="ada",
    )(c, w_ada, b_ada.reshape(1, N))


def _inproj_kernel(x_ref, mod_ref, w_ref, g0_ref, g1_ref, g2_ref, xl_ref, gl_ref, ga_ref, gr_ref, split_sc):
    tm = x_ref.shape[0]
    sh1 = mod_ref[0:1, :]
    sc1 = mod_ref[1:2, :]
    u = (x_ref[...] * (1.0 + sc1) + sh1).astype(BF16)

    def mm(c0, n):
        return jnp.dot(u, w_ref[:, c0:c0 + n], preferred_element_type=F32)

    n_split = [0]

    def store_group(g, which, res):
        ref = (g0_ref, g1_ref, g2_ref)[g]
        dil = DILATED_GROUPS[g][1]
        if dil == 1:
            ref[which] = res.astype(BF16)
            return
        buf = n_split[0] % split_sc.shape[0]
        n_split[0] += 1
        for s in range(SLABS):
            split_sc[buf, s] = res[:, s * LANES:(s + 1) * LANES]
        rows = tm // dil
        for r in range(dil):
            for s in range(SLABS):
                ref[which, r, :, s * LANES:(s + 1) * LANES] = (
                    split_sc[buf, s, pl.ds(r, rows, stride=dil), :].astype(BF16))

    for which in range(3):
        res = mm(which * ATTN_QKV, ATTN_QKV)
        if which == 0:
            res = res * (HEAD_DIM ** -0.5 * LOG2E)
        for g in range(N_GROUPS):
            store_group(g, which, res[:, g * GROUP_W:(g + 1) * GROUP_W])
    col = 3 * ATTN_QKV
    xl_ref[...] = mm(col, LRU_WIDTH)
    col += LRU_WIDTH
    gl_ref[...] = _gelu_tanh(mm(col, LRU_WIDTH)).astype(BF16)
    col += LRU_WIDTH
    for ref in (ga_ref, gr_ref):
        ref[...] = _sigmoid(mm(col, D_MODEL)).astype(BF16)
        col += D_MODEL


def _inproj(x, mod, w_in_b):
    B, S, D = x.shape
    tm = TM_IN
    ncols = w_in_b.shape[1]
    row = lambda b, i: (b, i, 0)
    bf_shape = jax.ShapeDtypeStruct((B, S, D), BF16)
    row_spec = pl.BlockSpec((None, tm, D), row)
    qkv_shapes, qkv_specs = [], []
    for _, dil in DILATED_GROUPS:
        if dil == 1:
            qkv_shapes.append(jax.ShapeDtypeStruct((3, B, S, GROUP_W), BF16))
            qkv_specs.append(pl.BlockSpec((3, None, tm, GROUP_W), lambda b, i: (0, b, i, 0)))
        else:
            qkv_shapes.append(jax.ShapeDtypeStruct((3, B, dil, S // dil, GROUP_W), BF16))
            qkv_specs.append(pl.BlockSpec((3, None, dil, tm // dil, GROUP_W), lambda b, i: (0, b, 0, i, 0)))
    return pl.pallas_call(
        _inproj_kernel,
        grid=(B, S // tm),
        in_specs=[row_spec,
                  pl.BlockSpec((None, 6, D), lambda b, i: (b, 0, 0)),
                  _resident((D, ncols), lambda b, i: (0, 0))],
        out_specs=qkv_specs + [row_spec, row_spec, row_spec, row_spec],
        out_shape=qkv_shapes + [jax.ShapeDtypeStruct((B, S, LRU_WIDTH), F32), bf_shape, bf_shape, bf_shape],
        scratch_shapes=[pltpu.VMEM((2, SLABS, tm, LANES), F32)],
        compiler_params=_cparams(("parallel", "parallel")),
        name="inproj",
    )(x, mod, w_in_b)


def _bias_kernel(tab_ref, bucket_ref, o_ref):
    hh = pl.program_id(0)
    bucket = bucket_ref[...]
    acc = jnp.zeros(bucket.shape, F32)
    for b in range(NUM_BUCKETS):
        acc = jnp.where(bucket == b, tab_ref[b, hh], acc)
    qi = lax.broadcasted_iota(jnp.int32, bucket.shape, 0)
    kj = lax.broadcasted_iota(jnp.int32, bucket.shape, 1)
    dist = qi + BLOCK - kj
    band = (dist >= 0) & (dist <= BLOCK)
    o_ref[...] = jnp.where(band, acc * LOG2E, -jnp.inf)


def _bias_tiles(rel_bias, buckets):
    nh = rel_bias.shape[1]
    return pl.pallas_call(
        _bias_kernel,
        grid=(nh,),
        in_specs=[pl.BlockSpec(memory_space=pltpu.SMEM),
                  pl.BlockSpec((None, BLOCK, 2 * BLOCK), lambda h: (h // HEADS_PER_GROUP, 0, 0))],
        out_specs=pl.BlockSpec((None, BLOCK, 2 * BLOCK), lambda h: (h, 0, 0)),
        out_shape=jax.ShapeDtypeStruct((nh, BLOCK, 2 * BLOCK), F32),
        compiler_params=_cparams(("arbitrary",)),
        name="bias_tiles",
    )(rel_bias, buckets)


def _bucket_maps():
    qi = jnp.arange(BLOCK)[:, None]
    kj = jnp.arange(2 * BLOCK)[None, :]
    dist = jnp.maximum(qi + BLOCK - kj, 0)
    maps = []
    for _, dil in DILATED_GROUPS:
        d = dist * dil
        nf = jnp.maximum(d, 1).astype(F32)
        large = MAX_EXACT + (jnp.log(nf / MAX_EXACT) / math.log(MAX_DISTANCE / MAX_EXACT)
                             * (NUM_BUCKETS - MAX_EXACT)).astype(jnp.int32)
        large = jnp.minimum(large, NUM_BUCKETS - 1)
        maps.append(jnp.where(d < MAX_EXACT, d, large))
    return jnp.stack(maps, 0).astype(jnp.int32)


def _attn_kernel(g0_ref, g1_ref, g2_ref, bias_ref, o_ref, part_sc):
    hp = pl.program_id(1)
    lane = lax.broadcasted_iota(jnp.int32, (BLOCK, PAIR_W), 1)
    first_head = lane < HEAD_DIM

    def core(g, qp, kp, vp, first):
        zero = jnp.zeros_like(qp)
        q2 = jnp.concatenate([jnp.where(first_head, qp, zero), jnp.where(first_head, zero, qp)], axis=0)
        bias2 = bias_ref[g * N_PAIRS + hp]
        if first:
            bias2 = bias2[:, BLOCK:]
        logits = lax.dot_general(q2, kp, (((1,), (1,)), ((), ())), preferred_element_type=F32) + bias2
        m = jnp.max(logits, axis=-1, keepdims=True)
        p = jnp.exp2(logits - m).astype(BF16)
        v1 = jnp.concatenate([vp, jnp.ones(vp.shape, BF16)], axis=1)
        pv = jnp.dot(p, v1, preferred_element_type=F32)
        pair = lambda t: jnp.where(first_head, t[:BLOCK], t[BLOCK:])
        return pair(m), pair(pv[:, PAIR_W:]), pair(pv[:, :PAIR_W])

    def qkv_block(rows_ref, r0, first):
        qp = rows_ref(0)[pl.ds(r0, BLOCK), :]
        if first:
            return qp, rows_ref(1)[pl.ds(r0, BLOCK), :], rows_ref(2)[pl.ds(r0, BLOCK), :]
        k0 = r0 - BLOCK if isinstance(r0, int) else pl.multiple_of(r0 - BLOCK, BLOCK)
        return qp, rows_ref(1)[pl.ds(k0, 2 * BLOCK), :], rows_ref(2)[pl.ds(k0, 2 * BLOCK), :]

    def dilated_group(g, ref):
        dil = DILATED_GROUPS[g][1]
        nblk = ref.shape[2] // BLOCK

        def body(r, carry):
            for n in range(nblk):
                qp, kp, vp = qkv_block(lambda w: ref.at[w, r], n * BLOCK, n == 0)
                parts = core(g, qp, kp, vp, n == 0)
                for j, t in enumerate(parts):
                    part_sc[3 * (g - 1) + j, pl.ds(n * BLOCK * dil + r, BLOCK, stride=dil), :] = t
            return carry
        lax.fori_loop(0, dil, body, 0, unroll=ATTN_UNROLL[g])

    dilated_group(1, g1_ref)
    dilated_group(2, g2_ref)

    def dense_block(r0, first):
        qp, kp, vp = qkv_block(lambda w: g0_ref.at[w], r0, first)
        m0, s0, pv0 = core(0, qp, kp, vp, first)
        rows = pl.ds(r0, BLOCK)
        m1, s1, pv1 = part_sc[0, rows, :], part_sc[1, rows, :], part_sc[2, rows, :]
        m2, s2, pv2 = part_sc[3, rows, :], part_sc[4, rows, :], part_sc[5, rows, :]
        mx = jnp.maximum(jnp.maximum(m0, m1), m2)
        a0, a1, a2 = jnp.exp2(m0 - mx), jnp.exp2(m1 - mx), jnp.exp2(m2 - mx)
        den = a0 * s0 + a1 * s1 + a2 * s2
        num = a0 * pv0 + a1 * pv1 + a2 * pv2
        o_ref[rows, :] = (num / den).astype(o_ref.dtype)

    dense_block(0, True)

    def body0(n, carry):
        dense_block(pl.multiple_of(n * BLOCK, BLOCK), False)
        return carry
    lax.fori_loop(1, g0_ref.shape[1] // BLOCK, body0, 0, unroll=ATTN_UNROLL[0])


def _attention(qkv0, qkv1, qkv2, bias):
    _, B, S, _ = qkv0.shape
    specs = [pl.BlockSpec((3, None, S, PAIR_W), lambda b, h: (0, b, 0, h))]
    for t in (qkv1, qkv2):
        dil, L = t.shape[2], t.shape[3]
        specs.append(pl.BlockSpec((3, None, dil, L, PAIR_W), lambda b, h: (0, b, 0, 0, h)))
    bias2 = bias.reshape(N_GROUPS * N_PAIRS, 2 * BLOCK, 2 * BLOCK)
    specs.append(_resident(bias2.shape, lambda b, h: (0, 0, 0)))
    return pl.pallas_call(
        _attn_kernel,
        grid=(B, N_PAIRS),
        in_specs=specs,
        out_specs=pl.BlockSpec((None, S, PAIR_W), lambda b, h: (b, 0, h)),
        out_shape=jax.ShapeDtypeStruct((B, S, GROUP_W), BF16),
        scratch_shapes=[pltpu.VMEM((3 * (N_GROUPS - 1), S, LANES), F32)],
        compiler_params=_cparams(("parallel", "parallel")),
        name="attn",
    )(qkv0, qkv1, qkv2, bias2)


def _lru_kernel(xl_ref, gl_ref, cw_ref, cb_ref, wg_ref, ba_ref, bx_ref, lam_ref, o_ref,
                xpad_sc, carry_sc, ta_sc, tb_sc, th_sc, tp_sc):
    ts, W = xl_ref.shape
    nseg = SUBLANES
    seg = ts // nseg
    nsl = W // LANES

    @pl.when(pl.program_id(1) == 0)
    def _():
        xpad_sc[0:HALO, :] = jnp.zeros((HALO, W), F32)
        carry_sc[...] = jnp.zeros(carry_sc.shape, F32)

    xpad_sc[HALO:HALO + ts, :] = xl_ref[...]
    xp = xpad_sc[...]
    t = cw_ref[0:1, :] * xp
    for kk in range(1, LRU_CONV):
        t = cw_ref[kk:kk + 1, :] * xp + pltpu.roll(t, 1, 0)
    xr = t[HALO:, :] + cb_ref[...]
    xpad_sc[0:HALO, :] = xl_ref[ts - HALO:ts, :]

    xb = xr.astype(BF16)
    ngrp = W // LRU_GRP
    zs = [jnp.dot(xb[:, c * LRU_GRP:(c + 1) * LRU_GRP], wg_ref[c], preferred_element_type=F32)
          for c in range(ngrp)]
    za = jnp.concatenate([z[:, :LRU_GRP] for z in zs], axis=1)
    zx = jnp.concatenate([z[:, LRU_GRP:] for z in zs], axis=1)
    r = _sigmoid(za + ba_ref[...])
    gi = _sigmoid(zx + bx_ref[...])
    nl = -lam_ref[...]
    softplus = jnp.maximum(nl, 0.0) + jnp.log1p(jnp.exp(-jnp.abs(nl)))
    log_a = r * ((-C_RGLRU) * softplus)
    a = jnp.exp(log_a)
    bt = jnp.sqrt(-jnp.tanh(log_a) * (a * a + 1.0)) * (gi * xr)

    for c in range(nsl):
        for i in range(nseg):
            dst = pl.ds(i, seg, stride=nseg)
            ta_sc[c, dst, :] = a[i * seg:(i + 1) * seg, c * LANES:(c + 1) * LANES]
            tb_sc[c, dst, :] = bt[i * seg:(i + 1) * seg, c * LANES:(c + 1) * LANES]

    def step(j, hp):
        h, p = hp
        rows = pl.ds(pl.multiple_of(j * nseg, nseg), nseg)
        aj = ta_sc[:, rows, :]
        h = aj * h + tb_sc[:, rows, :]
        p = aj * p
        th_sc[:, rows, :] = h
        tp_sc[:, rows, :] = p
        return h, p

    state = (jnp.zeros((nsl, nseg, LANES), F32), jnp.ones((nsl, nseg, LANES), F32))
    x_end, p_end = lax.fori_loop(0, seg, step, state, unroll=LRU_SCAN_UNROLL)

    sub = lax.broadcasted_iota(jnp.int32, (nsl, nseg, LANES), 1)
    sh = 1
    while sh < nseg:
        x_sh = jnp.where(sub >= sh, pltpu.roll(x_end, sh, 1), 0.0)
        p_sh = jnp.where(sub >= sh, pltpu.roll(p_end, sh, 1), 1.0)
        x_end = p_end * x_sh + x_end
        p_end = p_end * p_sh
        sh *= 2
    carry = carry_sc[...]
    leave = x_end + p_end * carry
    h_init = jnp.where(sub >= 1, pltpu.roll(leave, 1, 1), carry)
    carry_sc[...] = jnp.broadcast_to(leave[:, nseg - 1:nseg, :], carry_sc.shape)

    for c in range(nsl):
        for i in range(nseg):
            src = pl.ds(i, seg, stride=nseg)
            h = th_sc[c, src, :] + tp_sc[c, src, :] * h_init[c, i:i + 1, :]
            gate = gl_ref[i * seg:(i + 1) * seg, c * LANES:(c + 1) * LANES].astype(F32)
            o_ref[i * seg:(i + 1) * seg, c * LANES:(c + 1) * LANES] = (h * gate).astype(BF16)


def _lru(x_lru, gl, conv_w, conv_b, wg, ba, bx, lam):
    B, S, W = x_lru.shape
    ts = TS_LRU
    row = lambda b, i: (b, i, 0)
    const2 = lambda b, i: (0, 0)
    vec = pl.BlockSpec((1, W), const2)
    slabs = pltpu.VMEM((W // LANES, ts, LANES), F32)
    return pl.pallas_call(
        _lru_kernel,
        grid=(B, S // ts),
        in_specs=[pl.BlockSpec((None, ts, W), row), pl.BlockSpec((None, ts, W), row),
                  pl.BlockSpec((LRU_CONV, W), const2), vec,
                  pl.BlockSpec(wg.shape, lambda b, i: (0, 0, 0)), vec, vec, vec],
        out_specs=pl.BlockSpec((None, ts, W), row),
        out_shape=jax.ShapeDtypeStruct((B, S, W), BF16),
        scratch_shapes=[pltpu.VMEM((ts + HALO, W), F32), pltpu.VMEM((W // LANES, SUBLANES, LANES), F32),
                        slabs, slabs, slabs, slabs],
        compiler_params=_cparams(("parallel", "arbitrary")),
        name="lru",
    )(x_lru, gl, conv_w, conv_b.reshape(1, W), wg, ba.reshape(1, W), bx.reshape(1, W), lam.reshape(1, W))


def _lru_gate_weights(wa, wx):
    per = LRU_GRP // LRU_BLOCK_W
    ngrp = LRU_BLOCKS // per

    def bd(w):
        w = w.reshape(ngrp, per, LRU_BLOCK_W, LRU_BLOCK_W)
        eye = jnp.eye(per, dtype=w.dtype)
        t = jnp.einsum('gpcd,pq->gpcqd', w, eye)
        return t.reshape(ngrp, LRU_GRP, LRU_GRP)

    return jnp.concatenate([bd(wa), bd(wx)], axis=2).astype(BF16)


def _mix_kernel(x_ref, oa_ref, hg_ref, ga_ref, gr_ref, mod_ref, wpa_ref, wpl_ref, wo_ref,
                g_ref, b_ref, o_ref):
    g1 = mod_ref[2:3, :]
    tm = x_ref.shape[0]
    sub = tm // MIX_SPLIT
    for h in range(MIX_SPLIT):
        rows = slice(h * sub, (h + 1) * sub)
        ya = jnp.dot(oa_ref[rows, :], wpa_ref[...], preferred_element_type=F32)
        yl = jnp.dot(hg_ref[rows, :], wpl_ref[...], preferred_element_type=F32)
        merged = ga_ref[rows, :].astype(F32) * ya + gr_ref[rows, :].astype(F32) * yl
        mo = jnp.dot(merged.astype(BF16), wo_ref[...], preferred_element_type=F32)
        z = ALPHA * x_ref[rows, :] + g1 * mo
        o_ref[rows, :] = _layer_norm(z, g_ref[...], b_ref[...])


def _mix(x, o_attn, hg, ga, gr, mod, wpa, wpl, wo, ln_g, ln_b):
    B, S, D = x.shape
    tm = TM_MIX
    row = lambda b, i: (b, i, 0)
    const2 = lambda b, i: (0, 0)
    rspec = lambda w: pl.BlockSpec((None, tm, w), row)
    vec = pl.BlockSpec((1, D), const2)
    return pl.pallas_call(
        _mix_kernel,
        grid=(B, S // tm),
        in_specs=[rspec(D), rspec(GROUP_W), rspec(LRU_WIDTH), rspec(D), rspec(D),
                  pl.BlockSpec((None, 6, D), lambda b, i: (b, 0, 0)),
                  _resident(wpa.shape, const2), _resident(wpl.shape, const2), _resident(wo.shape, const2),
                  vec, vec],
        out_specs=rspec(D),
        out_shape=jax.ShapeDtypeStruct((B, S, D), F32),
        compiler_params=_cparams(("parallel", "parallel")),
        name="mix",
    )(x, o_attn, hg, ga, gr, mod, wpa, wpl, wo, ln_g.reshape(1, D), ln_b.reshape(1, D))


def _ffn_kernel(x_ref, mod_ref, wup_ref, cw_ref, cb_ref, wdn_ref, g_ref, b_ref, o_ref, carry_sc):
    tm = x_ref.shape[0]

    @pl.when(pl.program_id(1) == 0)
    def _():
        carry_sc[...] = jnp.zeros(carry_sc.shape, F32)

    x1 = x_ref[...]
    sh2 = mod_ref[3:4, :]
    sc2 = mod_ref[4:5, :]
    g2 = mod_ref[5:6, :]
    u2 = (x1 * (1.0 + sc2) + sh2).astype(BF16)

    def conv_up(c0, c1):
        h = jnp.dot(u2, wup_ref[:, c0:c1], preferred_element_type=F32)
        hx = jnp.concatenate([carry_sc[:, c0:c1], h], axis=0)
        carry_sc[:, c0:c1] = h[tm - HALO:tm, :]
        t = cw_ref[0:1, c0:c1] * hx
        for kk in range(1, FFN_CONV):
            t = cw_ref[kk:kk + 1, c0:c1] * hx + pltpu.roll(t, 1, 0)
        return t[HALO:, :] + cb_ref[:, c0:c1]

    acc = jnp.zeros((tm, D_MODEL), F32)
    c0 = 0
    for width in FF_CHUNKS:
        val = conv_up(c0, c0 + width)
        gate = conv_up(D_FF + c0, D_FF + c0 + width)
        act = (_gelu_tanh(gate) * val).astype(BF16)
        acc = acc + jnp.dot(act, wdn_ref[c0:c0 + width, :], preferred_element_type=F32)
        c0 += width
    z = ALPHA * x1 + g2 * acc
    o_ref[...] = _layer_norm(z, g_ref[...], b_ref[...])


def _ffn(x1, mod, wup, cw, cb, wdn, ln_g, ln_b):
    B, S, D = x1.shape
    tm = TM_FFN
    row = lambda b, i: (b, i, 0)
    const2 = lambda b, i: (0, 0)
    vec = pl.BlockSpec((1, D), const2)
    return pl.pallas_call(
        _ffn_kernel,
        grid=(B, S // tm),
        in_specs=[pl.BlockSpec((None, tm, D), row),
                  pl.BlockSpec((None, 6, D), lambda b, i: (b, 0, 0)),
                  _resident(wup.shape, const2), _resident(cw.shape, const2),
                  _resident(cb.shape, const2), _resident(wdn.shape, const2),
                  vec, vec],
        out_specs=pl.BlockSpec((None, tm, D), row),
        out_shape=jax.ShapeDtypeStruct((B, S, D), F32),
        scratch_shapes=[pltpu.VMEM((HALO, 2 * D_FF), F32)],
        compiler_params=_cparams(("parallel", "arbitrary")),
        name="ffn",
    )(x1, mod, wup, cw, cb, wdn, ln_g.reshape(1, D), ln_b.reshape(1, D))


def kernel(x, c, w_ada, b_ada, w_in, rel_bias, lru_conv_w, lru_conv_b, lru_wa, lru_ba, lru_wx, lru_bx,
           lru_lambda, w_proj_attn, w_proj_lru, w_out, ln1_g, ln1_b, ffn_w_up, ffn_conv_w, ffn_conv_b,
           ffn_w_down, ln2_g, ln2_b):
    B, S, D = x.shape
    assert w_ada.shape[0] == DEPTH == 1 and D == D_MODEL
    assert S == DILATED_GROUPS[-1][1] * BLOCK and S % TM_IN == 0
    assert all(window // dil == BLOCK for window, dil in DILATED_GROUPS)
    l = 0
    mod = _ada(c, w_ada[l], b_ada[l]).reshape(B, 6, D)

    qkv0, qkv1, qkv2, x_lru, gl, ga, gr = _inproj(x, mod, w_in[l].astype(BF16))

    bias = _bias_tiles(rel_bias, _bucket_maps())
    o_attn = _attention(qkv0, qkv1, qkv2, bias)

    wg = _lru_gate_weights(lru_wa[l], lru_wx[l])
    hg = _lru(x_lru, gl, lru_conv_w[l], lru_conv_b[l], wg, lru_ba[l], lru_bx[l], lru_lambda[l])

    x1 = _mix(x, o_attn, hg, ga, gr, mod, w_proj_attn[l].astype(BF16), w_proj_lru[l].astype(BF16),
              w_out[l].astype(BF16), ln1_g[l], ln1_b[l])

    return _ffn(x1, mod, ffn_w_up[l].astype(BF16), ffn_conv_w[l], ffn_conv_b[l][None, :],
                ffn_w_down[l].astype(BF16), ln2_g[l], ln2_b[l])
```

```python
import functools
import math

import jax
import jax.numpy as jnp
from jax import lax
from jax.experimental import pallas as pl
from jax.experimental.pallas import tpu as pltpu

F32 = jnp.float32
BF16 = jnp.bfloat16

D_MODEL = 1024
HEAD_DIM = 64
HEADS_PER_GROUP = 8
DILATED_GROUPS = ((128, 1), (512, 4), (2048, 16))
N_GROUPS = len(DILATED_GROUPS)
GROUP_W = HEADS_PER_GROUP * HEAD_DIM
ATTN_QKV = N_GROUPS * GROUP_W
BLOCK = 128
LOG2E = math.log2(math.e)
NUM_BUCKETS = 32
MAX_EXACT = NUM_BUCKETS // 2
MAX_DISTANCE = 2048
LRU_WIDTH = D_MODEL
LRU_BLOCKS = 16
LRU_BLOCK_W = LRU_WIDTH // LRU_BLOCKS
LRU_CONV = 4
C_RGLRU = 8.0
D_FF = 3 * D_MODEL
FFN_CONV = 3
DEPTH = 1
ALPHA = (2.0 * DEPTH) ** 0.25
LN_EPS = 1e-5

LANES = 128
SUBLANES = 8
VMEM_LIMIT = 56 * 1024 * 1024
TM_IN = 512
TM_MIX = 512
MIX_SPLIT = 2
TM_FFN = 512
LRU_SCAN_UNROLL = 4
FF_CHUNKS = (1536, 1536)
assert sum(FF_CHUNKS) == D_FF
LRU_GRP = 256
HALO = SUBLANES
PAIR_W = 2 * HEAD_DIM
N_PAIRS = GROUP_W // PAIR_W
SLABS = GROUP_W // LANES
ATTN_UNROLL = (15, 4, 16)


def _cparams(sem):
    return pltpu.CompilerParams(dimension_semantics=sem, vmem_limit_bytes=VMEM_LIMIT)


def _resident(shape, index_map):
    return pl.BlockSpec(shape, index_map, pipeline_mode=pl.Buffered(1))


def _gelu_tanh(x):
    return 0.5 * x * (1.0 + jnp.tanh(math.sqrt(2.0 / math.pi) * (x + 0.044715 * (x * x * x))))


def _sigmoid(x):
    return 0.5 * jnp.tanh(0.5 * x) + 0.5


def _layer_norm(z, g, b):
    mu = jnp.mean(z, axis=-1, keepdims=True)
    zc = z - mu
    var = jnp.mean(zc * zc, axis=-1, keepdims=True)
    return zc * lax.rsqrt(var + LN_EPS) * g + b


def _ada_kernel(c_ref, w_ref, b_ref, o_ref):
    c = c_ref[...]
    ca = (c * _sigmoid(c)).astype(BF16)
    o_ref[...] = jnp.dot(ca, w_ref[...].astype(BF16), preferred_element_type=F32) + b_ref[...]


def _ada(c, w_ada, b_ada):
    B, D = c.shape
    N = w_ada.shape[1]
    tn = D
    return pl.pallas_call(
        _ada_kernel,
        grid=(N // tn,),
        in_specs=[pl.BlockSpec((B, D), lambda j: (0, 0)),
                  pl.BlockSpec((D, tn), lambda j: (0, j)),
                  pl.BlockSpec((1, tn), lambda j: (0, j))],
        out_specs=pl.BlockSpec((B, tn), lambda j: (0, j)),
        out_shape=jax.ShapeDtypeStruct((B, N), F32),
        compiler_params=_cparams(("arbitrary",)),
        name="ada",
    )(c, w_ada, b_ada.reshape(1, N))


def _inproj_kernel(x_ref, mod_ref, w_ref, cw_ref, cb_ref, wg_ref, ba_ref, bx_ref, lam_ref,
                   g0_ref, g1_ref, g2_ref, hg_ref, ga_ref, gr_ref,
                   split_sc, xpad_sc, carry_sc, ta_sc, tb_sc):
    tm = x_ref.shape[0]
    W = LRU_WIDTH
    nseg = SUBLANES
    seg = tm // nseg
    nsl = W // LANES
    sh1 = mod_ref[0:1, :]
    sc1 = mod_ref[1:2, :]
    u = (x_ref[...] * (1.0 + sc1) + sh1).astype(BF16)

    def mm(c0, n):
        return jnp.dot(u, w_ref[:, c0:c0 + n], preferred_element_type=F32)

    @pl.when(pl.program_id(1) == 0)
    def _():
        xpad_sc[0:HALO, :] = jnp.zeros((HALO, W), F32)
        carry_sc[...] = jnp.zeros(carry_sc.shape, F32)

    col_lru = 3 * ATTN_QKV
    xl = mm(col_lru, W)
    xpad_sc[HALO:HALO + tm, :] = xl
    xp = xpad_sc[...]
    t = cw_ref[0:1, :] * xp
    for kk in range(1, LRU_CONV):
        t = cw_ref[kk:kk + 1, :] * xp + pltpu.roll(t, 1, 0)
    xr = t[HALO:, :] + cb_ref[...]
    xpad_sc[0:HALO, :] = xl[tm - HALO:tm, :]

    n_split = [0]

    def store_group(g, which, res):
        ref = (g0_ref, g1_ref, g2_ref)[g]
        dil = DILATED_GROUPS[g][1]
        if dil == 1:
            ref[which] = res.astype(BF16)
            return
        buf = n_split[0] % split_sc.shape[0]
        n_split[0] += 1
        for s in range(SLABS):
            split_sc[buf, s] = res[:, s * LANES:(s + 1) * LANES]
        rows = tm // dil
        for r in range(dil):
            for s in range(SLABS):
                ref[which, r, :, s * LANES:(s + 1) * LANES] = (
                    split_sc[buf, s, pl.ds(r, rows, stride=dil), :].astype(BF16))

    def qkv(which):
        res = mm(which * ATTN_QKV, ATTN_QKV)
        if which == 0:
            res = res * (HEAD_DIM ** -0.5 * LOG2E)
        for g in range(N_GROUPS):
            store_group(g, which, res[:, g * GROUP_W:(g + 1) * GROUP_W])

    qkv(0)
    xb = xr.astype(BF16)
    zs = [jnp.dot(xb[:, c * LRU_GRP:(c + 1) * LRU_GRP], wg_ref[c], preferred_element_type=F32)
          for c in range(W // LRU_GRP)]
    za = jnp.concatenate([z[:, :LRU_GRP] for z in zs], axis=1)
    zx = jnp.concatenate([z[:, LRU_GRP:] for z in zs], axis=1)
    r = _sigmoid(za + ba_ref[...])
    gi = _sigmoid(zx + bx_ref[...])
    nl = -lam_ref[...]
    softplus = jnp.maximum(nl, 0.0) + jnp.log1p(jnp.exp(-jnp.abs(nl)))
    log_a = r * ((-C_RGLRU) * softplus)
    a = jnp.exp(log_a)
    bt = jnp.sqrt(-jnp.tanh(log_a) * (a * a + 1.0)) * (gi * xr)
    for c in range(nsl):
        for i in range(nseg):
            dst = pl.ds(i, seg, stride=nseg)
            ta_sc[c, dst, :] = a[i * seg:(i + 1) * seg, c * LANES:(c + 1) * LANES]
            tb_sc[c, dst, :] = bt[i * seg:(i + 1) * seg, c * LANES:(c + 1) * LANES]
    qkv(1)
    qkv(2)

    gate = _gelu_tanh(mm(col_lru + W, W)).astype(BF16)
    col = col_lru + 2 * W
    for ref in (ga_ref, gr_ref):
        ref[...] = _sigmoid(mm(col, D_MODEL)).astype(BF16)
        col += D_MODEL

    def step(j, hp):
        h, p = hp
        rows = pl.ds(pl.multiple_of(j * nseg, nseg), nseg)
        aj = ta_sc[:, rows, :]
        h = aj * h + tb_sc[:, rows, :]
        p = aj * p
        tb_sc[:, rows, :] = h
        ta_sc[:, rows, :] = p
        return h, p

    state = (jnp.zeros((nsl, nseg, LANES), F32), jnp.ones((nsl, nseg, LANES), F32))
    x_end, p_end = lax.fori_loop(0, seg, step, state, unroll=LRU_SCAN_UNROLL)

    sub = lax.broadcasted_iota(jnp.int32, (nsl, nseg, LANES), 1)
    sh = 1
    while sh < nseg:
        x_sh = jnp.where(sub >= sh, pltpu.roll(x_end, sh, 1), 0.0)
        p_sh = jnp.where(sub >= sh, pltpu.roll(p_end, sh, 1), 1.0)
        x_end = p_end * x_sh + x_end
        p_end = p_end * p_sh
        sh *= 2
    carry = carry_sc[...]
    leave = x_end + p_end * carry
    h_init = jnp.where(sub >= 1, pltpu.roll(leave, 1, 1), carry)
    carry_sc[...] = jnp.broadcast_to(leave[:, nseg - 1:nseg, :], carry_sc.shape)

    for c in range(nsl):
        for i in range(nseg):
            src = pl.ds(i, seg, stride=nseg)
            h = tb_sc[c, src, :] + ta_sc[c, src, :] * h_init[c, i:i + 1, :]
            g = gate[i * seg:(i + 1) * seg, c * LANES:(c + 1) * LANES].astype(F32)
            hg_ref[i * seg:(i + 1) * seg, c * LANES:(c + 1) * LANES] = (h * g).astype(BF16)


def _inproj(x, mod, w_in_b, conv_w, conv_b, wg, ba, bx, lam):
    B, S, D = x.shape
    tm = TM_IN
    W = LRU_WIDTH
    ncols = w_in_b.shape[1]
    row = lambda b, i: (b, i, 0)
    const2 = lambda b, i: (0, 0)
    vec = pl.BlockSpec((1, W), const2)
    bf_shape = jax.ShapeDtypeStruct((B, S, D), BF16)
    row_spec = pl.BlockSpec((None, tm, D), row)
    slabs = pltpu.VMEM((W // LANES, tm, LANES), F32)
    qkv_shapes, qkv_specs = [], []
    for _, dil in DILATED_GROUPS:
        if dil == 1:
            qkv_shapes.append(jax.ShapeDtypeStruct((3, B, S, GROUP_W), BF16))
            qkv_specs.append(pl.BlockSpec((3, None, tm, GROUP_W), lambda b, i: (0, b, i, 0)))
        else:
            qkv_shapes.append(jax.ShapeDtypeStruct((3, B, dil, S // dil, GROUP_W), BF16))
            qkv_specs.append(pl.BlockSpec((3, None, dil, tm // dil, GROUP_W), lambda b, i: (0, b, 0, i, 0)))
    return pl.pallas_call(
        _inproj_kernel,
        grid=(B, S // tm),
        in_specs=[row_spec,
                  pl.BlockSpec((None, 6, D), lambda b, i: (b, 0, 0)),
                  _resident((D, ncols), const2),
                  pl.BlockSpec((LRU_CONV, W), const2), vec,
                  _resident(wg.shape, lambda b, i: (0, 0, 0)), vec, vec, vec],
        out_specs=qkv_specs + [row_spec, row_spec, row_spec],
        out_shape=qkv_shapes + [bf_shape, bf_shape, bf_shape],
        scratch_shapes=[pltpu.VMEM((2, SLABS, tm, LANES), F32),
                        pltpu.VMEM((tm + HALO, W), F32), pltpu.VMEM((W // LANES, SUBLANES, LANES), F32),
                        slabs, slabs],
        compiler_params=_cparams(("parallel", "arbitrary")),
        name="inproj",
    )(x, mod, w_in_b, conv_w, conv_b.reshape(1, W), wg, ba.reshape(1, W), bx.reshape(1, W), lam.reshape(1, W))


def _bias_kernel(tab_ref, bucket_ref, o_ref):
    hh = pl.program_id(0)
    bucket = bucket_ref[...]
    acc = jnp.zeros(bucket.shape, F32)
    for b in range(NUM_BUCKETS):
        acc = jnp.where(bucket == b, tab_ref[b, hh], acc)
    qi = lax.broadcasted_iota(jnp.int32, bucket.shape, 0)
    kj = lax.broadcasted_iota(jnp.int32, bucket.shape, 1)
    dist = qi + BLOCK - kj
    band = (dist >= 0) & (dist <= BLOCK)
    o_ref[...] = jnp.where(band, acc * LOG2E, -jnp.inf)


def _bias_tiles(rel_bias, buckets):
    nh = rel_bias.shape[1]
    return pl.pallas_call(
        _bias_kernel,
        grid=(nh,),
        in_specs=[pl.BlockSpec(memory_space=pltpu.SMEM),
                  pl.BlockSpec((None, BLOCK, 2 * BLOCK), lambda h: (h // HEADS_PER_GROUP, 0, 0))],
        out_specs=pl.BlockSpec((None, BLOCK, 2 * BLOCK), lambda h: (h, 0, 0)),
        out_shape=jax.ShapeDtypeStruct((nh, BLOCK, 2 * BLOCK), F32),
        compiler_params=_cparams(("arbitrary",)),
        name="bias_tiles",
    )(rel_bias, buckets)


def _bucket_maps():
    qi = jnp.arange(BLOCK)[:, None]
    kj = jnp.arange(2 * BLOCK)[None, :]
    dist = jnp.maximum(qi + BLOCK - kj, 0)
    maps = []
    for _, dil in DILATED_GROUPS:
        d = dist * dil
        nf = jnp.maximum(d, 1).astype(F32)
        large = MAX_EXACT + (jnp.log(nf / MAX_EXACT) / math.log(MAX_DISTANCE / MAX_EXACT)
                             * (NUM_BUCKETS - MAX_EXACT)).astype(jnp.int32)
        large = jnp.minimum(large, NUM_BUCKETS - 1)
        maps.append(jnp.where(d < MAX_EXACT, d, large))
    return jnp.stack(maps, 0).astype(jnp.int32)


def _attn_kernel(g0_ref, g1_ref, g2_ref, bias_ref, o_ref, part_sc):
    hp = pl.program_id(1)
    lane = lax.broadcasted_iota(jnp.int32, (BLOCK, PAIR_W), 1)
    first_head = lane < HEAD_DIM

    def core(g, qp, kp, vp, first):
        zero = jnp.zeros_like(qp)
        q2 = jnp.concatenate([jnp.where(first_head, qp, zero), jnp.where(first_head, zero, qp)], axis=0)
        bias2 = bias_ref[g * N_PAIRS + hp]
        if first:
            bias2 = bias2[:, BLOCK:]
        logits = lax.dot_general(q2, kp, (((1,), (1,)), ((), ())), preferred_element_type=F32) + bias2
        m = jnp.max(logits, axis=-1, keepdims=True)
        p = jnp.exp2(logits - m).astype(BF16)
        v1 = jnp.concatenate([vp, jnp.ones(vp.shape, BF16)], axis=1)
        pv = jnp.dot(p, v1, preferred_element_type=F32)
        pair = lambda t: jnp.where(first_head, t[:BLOCK], t[BLOCK:])
        return pair(m), pair(pv[:, PAIR_W:]), pair(pv[:, :PAIR_W])

    def qkv_block(rows_ref, r0, first):
        qp = rows_ref(0)[pl.ds(r0, BLOCK), :]
        if first:
            return qp, rows_ref(1)[pl.ds(r0, BLOCK), :], rows_ref(2)[pl.ds(r0, BLOCK), :]
        k0 = r0 - BLOCK if isinstance(r0, int) else pl.multiple_of(r0 - BLOCK, BLOCK)
        return qp, rows_ref(1)[pl.ds(k0, 2 * BLOCK), :], rows_ref(2)[pl.ds(k0, 2 * BLOCK), :]

    def dilated_group(g, ref):
        dil = DILATED_GROUPS[g][1]
        nblk = ref.shape[2] // BLOCK

        def body(r, carry):
            for n in range(nblk):
                qp, kp, vp = qkv_block(lambda w: ref.at[w, r], n * BLOCK, n == 0)
                parts = core(g, qp, kp, vp, n == 0)
                for j, t in enumerate(parts):
                    part_sc[3 * (g - 1) + j, pl.ds(n * BLOCK * dil + r, BLOCK, stride=dil), :] = t
            return carry
        lax.fori_loop(0, dil, body, 0, unroll=ATTN_UNROLL[g])

    dilated_group(1, g1_ref)
    dilated_group(2, g2_ref)

    def dense_block(r0, first):
        qp, kp, vp = qkv_block(lambda w: g0_ref.at[w], r0, first)
        m0, s0, pv0 = core(0, qp, kp, vp, first)
        rows = pl.ds(r0, BLOCK)
        m1, s1, pv1 = part_sc[0, rows, :], part_sc[1, rows, :], part_sc[2, rows, :]
        m2, s2, pv2 = part_sc[3, rows, :], part_sc[4, rows, :], part_sc[5, rows, :]
        mx = jnp.maximum(jnp.maximum(m0, m1), m2)
        a0, a1, a2 = jnp.exp2(m0 - mx), jnp.exp2(m1 - mx), jnp.exp2(m2 - mx)
        den = a0 * s0 + a1 * s1 + a2 * s2
        num = a0 * pv0 + a1 * pv1 + a2 * pv2
        o_ref[rows, :] = (num / den).astype(o_ref.dtype)

    dense_block(0, True)

    def body0(n, carry):
        dense_block(pl.multiple_of(n * BLOCK, BLOCK), False)
        return carry
    lax.fori_loop(1, g0_ref.shape[1] // BLOCK, body0, 0, unroll=ATTN_UNROLL[0])


def _attention(qkv0, qkv1, qkv2, bias):
    _, B, S, _ = qkv0.shape
    specs = [pl.BlockSpec((3, None, S, PAIR_W), lambda b, h: (0, b, 0, h))]
    for t in (qkv1, qkv2):
        dil, L = t.shape[2], t.shape[3]
        specs.append(pl.BlockSpec((3, None, dil, L, PAIR_W), lambda b, h: (0, b, 0, 0, h)))
    bias2 = bias.reshape(N_GROUPS * N_PAIRS, 2 * BLOCK, 2 * BLOCK)
    specs.append(_resident(bias2.shape, lambda b, h: (0, 0, 0)))
    return pl.pallas_call(
        _attn_kernel,
        grid=(B, N_PAIRS),
        in_specs=specs,
        out_specs=pl.BlockSpec((None, S, PAIR_W), lambda b, h: (b, 0, h)),
        out_shape=jax.ShapeDtypeStruct((B, S, GROUP_W), BF16),
        scratch_shapes=[pltpu.VMEM((3 * (N_GROUPS - 1), S, LANES), F32)],
        compiler_params=_cparams(("parallel", "parallel")),
        name="attn",
    )(qkv0, qkv1, qkv2, bias2)


def _lru_gate_weights(wa, wx):
    per = LRU_GRP // LRU_BLOCK_W
    ngrp = LRU_BLOCKS // per

    def bd(w):
        w = w.reshape(ngrp, per, LRU_BLOCK_W, LRU_BLOCK_W)
        eye = jnp.eye(per, dtype=w.dtype)
        t = jnp.einsum('gpcd,pq->gpcqd', w, eye)
        return t.reshape(ngrp, LRU_GRP, LRU_GRP)

    return jnp.concatenate([bd(wa), bd(wx)], axis=2).astype(BF16)


def _mix_kernel(x_ref, oa_ref, hg_ref, ga_ref, gr_ref, mod_ref, wpa_ref, wpl_ref, wo_ref,
                g_ref, b_ref, o_ref):
    g1 = mod_ref[2:3, :]
    tm = x_ref.shape[0]
    sub = tm // MIX_SPLIT
    for h in range(MIX_SPLIT):
        rows = slice(h * sub, (h + 1) * sub)
        ya = jnp.dot(oa_ref[rows, :], wpa_ref[...], preferred_element_type=F32)
        yl = jnp.dot(hg_ref[rows, :], wpl_ref[...], preferred_element_type=F32)
        merged = ga_ref[rows, :].astype(F32) * ya + gr_ref[rows, :].astype(F32) * yl
        mo = jnp.dot(merged.astype(BF16), wo_ref[...], preferred_element_type=F32)
        z = ALPHA * x_ref[rows, :] + g1 * mo
        o_ref[rows, :] = _layer_norm(z, g_ref[...], b_ref[...])


def _mix(x, o_attn, hg, ga, gr, mod, wpa, wpl, wo, ln_g, ln_b):
    B, S, D = x.shape
    tm = TM_MIX
    row = lambda b, i: (b, i, 0)
    const2 = lambda b, i: (0, 0)
    rspec = lambda w: pl.BlockSpec((None, tm, w), row)
    vec = pl.BlockSpec((1, D), const2)
    return pl.pallas_call(
        _mix_kernel,
        grid=(B, S // tm),
        in_specs=[rspec(D), rspec(GROUP_W), rspec(LRU_WIDTH), rspec(D), rspec(D),
                  pl.BlockSpec((None, 6, D), lambda b, i: (b, 0, 0)),
                  _resident(wpa.shape, const2), _resident(wpl.shape, const2), _resident(wo.shape, const2),
                  vec, vec],
        out_specs=rspec(D),
        out_shape=jax.ShapeDtypeStruct((B, S, D), F32),
        compiler_params=_cparams(("parallel", "parallel")),
        name="mix",
    )(x, o_attn, hg, ga, gr, mod, wpa, wpl, wo, ln_g.reshape(1, D), ln_b.reshape(1, D))


def _ffn_kernel(x_ref, mod_ref, wup_ref, cw_ref, cb_ref, wdn_ref, g_ref, b_ref, o_ref, perm_sc, carry_sc):
    tm = x_ref.shape[0]
    nseg = SUBLANES
    seg = tm // nseg
    nsl = D_MODEL // LANES
    back = FFN_CONV - 1

    @pl.when(pl.program_id(1) == 0)
    def _():
        carry_sc[...] = jnp.zeros(carry_sc.shape, F32)

    for c in range(nsl):
        for i in range(nseg):
            perm_sc[c, pl.ds(i, seg, stride=nseg), :] = x_ref[i * seg:(i + 1) * seg, c * LANES:(c + 1) * LANES]
    x1 = jnp.concatenate([perm_sc[c] for c in range(nsl)], axis=1)
    sh2 = mod_ref[3:4, :]
    sc2 = mod_ref[4:5, :]
    g2 = mod_ref[5:6, :]
    u2 = (x1 * (1.0 + sc2) + sh2).astype(BF16)

    def conv_up(c0, c1):
        h = jnp.dot(u2, wup_ref[:, c0:c1], preferred_element_type=F32)
        sub = lax.broadcasted_iota(jnp.int32, (nseg, c1 - c0), 0)
        head = []
        for v in range(back):
            rows = slice(tm - (back - v) * nseg, tm - (back - v - 1) * nseg)
            prev = carry_sc[v * nseg:(v + 1) * nseg, c0:c1]
            head.append(jnp.where(sub >= 1, pltpu.roll(h[rows, :], 1, 0), pltpu.roll(prev, 1, 0)))
            carry_sc[v * nseg:(v + 1) * nseg, c0:c1] = h[rows, :]
        hx = jnp.concatenate(head + [h], axis=0)
        y = cb_ref[:, c0:c1] + cw_ref[back:back + 1, c0:c1] * h
        for kk in range(back):
            y = y + cw_ref[kk:kk + 1, c0:c1] * hx[kk * nseg:kk * nseg + tm, :]
        return y

    acc = jnp.zeros((tm, D_MODEL), F32)
    c0 = 0
    for width in FF_CHUNKS:
        val = conv_up(c0, c0 + width)
        gate = conv_up(D_FF + c0, D_FF + c0 + width)
        act = (_gelu_tanh(gate) * val).astype(BF16)
        acc = acc + jnp.dot(act, wdn_ref[c0:c0 + width, :], preferred_element_type=F32)
        c0 += width
    z = ALPHA * x1 + g2 * acc
    out = _layer_norm(z, g_ref[...], b_ref[...])
    for c in range(nsl):
        perm_sc[c] = out[:, c * LANES:(c + 1) * LANES]
    for c in range(nsl):
        for i in range(nseg):
            o_ref[i * seg:(i + 1) * seg, c * LANES:(c + 1) * LANES] = perm_sc[c, pl.ds(i, seg, stride=nseg), :]


def _ffn(x1, mod, wup, cw, cb, wdn, ln_g, ln_b):
    B, S, D = x1.shape
    tm = TM_FFN
    row = lambda b, i: (b, i, 0)
    const2 = lambda b, i: (0, 0)
    vec = pl.BlockSpec((1, D), const2)
    return pl.pallas_call(
        _ffn_kernel,
        grid=(B, S // tm),
        in_specs=[pl.BlockSpec((None, tm, D), row),
                  pl.BlockSpec((None, 6, D), lambda b, i: (b, 0, 0)),
                  _resident(wup.shape, const2), _resident(cw.shape, const2),
                  _resident(cb.shape, const2), _resident(wdn.shape, const2),
                  vec, vec],
        out_specs=pl.BlockSpec((None, tm, D), row),
        out_shape=jax.ShapeDtypeStruct((B, S, D), F32),
        scratch_shapes=[pltpu.VMEM((D // LANES, tm, LANES), F32),
                        pltpu.VMEM(((FFN_CONV - 1) * SUBLANES, 2 * D_FF), F32)],
        compiler_params=_cparams(("parallel", "arbitrary")),
        name="ffn",
    )(x1, mod, wup, cw, cb, wdn, ln_g.reshape(1, D), ln_b.reshape(1, D))


def kernel(x, c, w_ada, b_ada, w_in, rel_bias, lru_conv_w, lru_conv_b, lru_wa, lru_ba, lru_wx, lru_bx,
           lru_lambda, w_proj_attn, w_proj_lru, w_out, ln1_g, ln1_b, ffn_w_up, ffn_conv_w, ffn_conv_b,
           ffn_w_down, ln2_g, ln2_b):
    B, S, D = x.shape
    assert w_ada.shape[0] == DEPTH == 1 and D == D_MODEL
    assert S == DILATED_GROUPS[-1][1] * BLOCK and S % TM_IN == 0
    assert all(window // dil == BLOCK for window, dil in DILATED_GROUPS)
    l = 0
    mod = _ada(c, w_ada[l], b_ada[l]).reshape(B, 6, D)

    wg = _lru_gate_weights(lru_wa[l], lru_wx[l])
    qkv0, qkv1, qkv2, hg, ga, gr = _inproj(x, mod, w_in[l].astype(BF16), lru_conv_w[l], lru_conv_b[l], wg,
                                           lru_ba[l], lru_bx[l], lru_lambda[l])

    bias = _bias_tiles(rel_bias, _bucket_maps())
    o_attn = _attention(qkv0, qkv1, qkv2, bias)

    x1 = _mix(x, o_attn, hg, ga, gr, mod, w_proj_attn[l].astype(BF16), w_proj_lru[l].astype(BF16),
              w_out[l].astype(BF16), ln1_g[l], ln1_b[l])

    return _ffn(x1, mod, ffn_w_up[l].astype(BF16), ffn_conv_w[l], ffn_conv_b[l][None, :],
                ffn_w_down[l].astype(BF16), ln2_g[l], ln2_b[l])
```

```python
import functools
import math

import jax
import jax.numpy as jnp
from jax import lax
from jax.experimental import pallas as pl
from jax.experimental.pallas import tpu as pltpu

F32 = jnp.float32
BF16 = jnp.bfloat16

D_MODEL = 1024
HEAD_DIM = 64
HEADS_PER_GROUP = 8
DILATED_GROUPS = ((128, 1), (512, 4), (2048, 16))
N_GROUPS = len(DILATED_GROUPS)
GROUP_W = HEADS_PER_GROUP * HEAD_DIM
ATTN_QKV = N_GROUPS * GROUP_W
BLOCK = 128
LOG2E = math.log2(math.e)
NUM_BUCKETS = 32
MAX_EXACT = NUM_BUCKETS // 2
MAX_DISTANCE = 2048
LRU_WIDTH = D_MODEL
LRU_BLOCKS = 16
LRU_BLOCK_W = LRU_WIDTH // LRU_BLOCKS
LRU_CONV = 4
C_RGLRU = 8.0
D_FF = 3 * D_MODEL
FFN_CONV = 3
DEPTH = 1
ALPHA = (2.0 * DEPTH) ** 0.25
LN_EPS = 1e-5

LANES = 128
SUBLANES = 8
VMEM_LIMIT = 56 * 1024 * 1024
TM_IN = 512
TM_MIX = 512
MIX_SPLIT = 2
TM_FFN = 512
LRU_SCAN_UNROLL = 4
FF_CHUNKS = (1536, 1536)
assert sum(FF_CHUNKS) == D_FF
LRU_GRP = 256
HALO = SUBLANES
PAIR_W = 2 * HEAD_DIM
N_PAIRS = GROUP_W // PAIR_W
SLABS = GROUP_W // LANES
ATTN_UNROLL = (15, 4, 16)


def _cparams(sem):
    return pltpu.CompilerParams(dimension_semantics=sem, vmem_limit_bytes=VMEM_LIMIT)


def _resident(shape, index_map):
    return pl.BlockSpec(shape, index_map, pipeline_mode=pl.Buffered(1))


def _gelu_tanh(x):
    return 0.5 * x * (1.0 + jnp.tanh(math.sqrt(2.0 / math.pi) * (x + 0.044715 * (x * x * x))))


def _sigmoid(x):
    return 0.5 * jnp.tanh(0.5 * x) + 0.5


def _layer_norm(z, g, b):
    mu = jnp.mean(z, axis=-1, keepdims=True)
    zc = z - mu
    var = jnp.mean(zc * zc, axis=-1, keepdims=True)
    return zc * lax.rsqrt(var + LN_EPS) * g + b


def _ada_kernel(c_ref, w_ref, b_ref, o_ref):
    c = c_ref[...]
    ca = (c * _sigmoid(c)).astype(BF16)
    o_ref[...] = jnp.dot(ca, w_ref[...].astype(BF16), preferred_element_type=F32) + b_ref[...]


def _ada(c, w_ada, b_ada):
    B, D = c.shape
    N = w_ada.shape[1]
    tn = D
    return pl.pallas_call(
        _ada_kernel,
        grid=(N // tn,),
        in_specs=[pl.BlockSpec((B, D), lambda j: (0, 0)),
                  pl.BlockSpec((D, tn), lambda j: (0, j)),
                  pl.BlockSpec((1, tn), lambda j: (0, j))],
        out_specs=pl.BlockSpec((B, tn), lambda j: (0, j)),
        out_shape=jax.ShapeDtypeStruct((B, N), F32),
        compiler_params=_cparams(("arbitrary",)),
        name="ada",
    )(c, w_ada, b_ada.reshape(1, N))


def _inproj_kernel(x_ref, mod_ref, w_ref, cw_ref, cb_ref, wg_ref, ba_ref, bx_ref, lam_ref,
                   g0_ref, g1_ref, g2_ref, hg_ref, ga_ref, gr_ref,
                   split_sc, xpad_sc, carry_sc, ta_sc, tb_sc):
    tm = x_ref.shape[0]
    W = LRU_WIDTH
    nseg = SUBLANES
    seg = tm // nseg
    nsl = W // LANES
    sh1 = mod_ref[0:1, :]
    sc1 = mod_ref[1:2, :]
    u = (x_ref[...] * (1.0 + sc1) + sh1).astype(BF16)

    def mm(c0, n):
        return jnp.dot(u, w_ref[:, c0:c0 + n], preferred_element_type=F32)

    @pl.when(pl.program_id(1) == 0)
    def _():
        xpad_sc[0:HALO, :] = jnp.zeros((HALO, W), F32)
        carry_sc[...] = jnp.zeros(carry_sc.shape, F32)

    col_lru = 3 * ATTN_QKV
    xl = mm(col_lru, W)
    xpad_sc[HALO:HALO + tm, :] = xl
    xp = xpad_sc[...]
    t = cw_ref[0:1, :] * xp
    for kk in range(1, LRU_CONV):
        t = cw_ref[kk:kk + 1, :] * xp + pltpu.roll(t, 1, 0)
    xr = t[HALO:, :] + cb_ref[...]
    xpad_sc[0:HALO, :] = xl[tm - HALO:tm, :]

    n_split = [0]

    def store_group(g, which, res):
        ref = (g0_ref, g1_ref, g2_ref)[g]
        dil = DILATED_GROUPS[g][1]
        if dil == 1:
            ref[which] = res.astype(BF16)
            return
        buf = n_split[0] % split_sc.shape[0]
        n_split[0] += 1
        for s in range(SLABS):
            split_sc[buf, s] = res[:, s * LANES:(s + 1) * LANES]
        rows = tm // dil
        for r in range(dil):
            for s in range(SLABS):
                ref[which, r, :, s * LANES:(s + 1) * LANES] = (
                    split_sc[buf, s, pl.ds(r, rows, stride=dil), :].astype(BF16))

    def qkv(which):
        res = mm(which * ATTN_QKV, ATTN_QKV)
        if which == 0:
            res = res * (HEAD_DIM ** -0.5 * LOG2E)
        for g in range(N_GROUPS):
            store_group(g, which, res[:, g * GROUP_W:(g + 1) * GROUP_W])

    qkv(0)
    xb = xr.astype(BF16)
    zs = [jnp.dot(xb[:, c * LRU_GRP:(c + 1) * LRU_GRP], wg_ref[c], preferred_element_type=F32)
          for c in range(W // LRU_GRP)]
    za = jnp.concatenate([z[:, :LRU_GRP] for z in zs], axis=1)
    zx = jnp.concatenate([z[:, LRU_GRP:] for z in zs], axis=1)
    r = _sigmoid(za + ba_ref[...])
    gi = _sigmoid(zx + bx_ref[...])
    nl = -lam_ref[...]
    softplus = jnp.maximum(nl, 0.0) + jnp.log1p(jnp.exp(-jnp.abs(nl)))
    log_a = r * ((-C_RGLRU) * softplus)
    a = jnp.exp(log_a)
    bt = jnp.sqrt(-jnp.tanh(log_a) * (a * a + 1.0)) * (gi * xr)
    for c in range(nsl):
        for i in range(nseg):
            dst = pl.ds(i, seg, stride=nseg)
            ta_sc[c, dst, :] = a[i * seg:(i + 1) * seg, c * LANES:(c + 1) * LANES]
            tb_sc[c, dst, :] = bt[i * seg:(i + 1) * seg, c * LANES:(c + 1) * LANES]
    qkv(1)
    qkv(2)

    gate = _gelu_tanh(mm(col_lru + W, W)).astype(BF16)
    col = col_lru + 2 * W
    for ref in (ga_ref, gr_ref):
        ref[...] = _sigmoid(mm(col, D_MODEL)).astype(BF16)
        col += D_MODEL

    def step(j, hp):
        h, p = hp
        rows = pl.ds(pl.multiple_of(j * nseg, nseg), nseg)
        aj = ta_sc[:, rows, :]
        h = aj * h + tb_sc[:, rows, :]
        p = aj * p
        tb_sc[:, rows, :] = h
        ta_sc[:, rows, :] = p
        return h, p

    state = (jnp.zeros((nsl, nseg, LANES), F32), jnp.ones((nsl, nseg, LANES), F32))
    x_end, p_end = lax.fori_loop(0, seg, step, state, unroll=LRU_SCAN_UNROLL)

    sub = lax.broadcasted_iota(jnp.int32, (nsl, nseg, LANES), 1)
    sh = 1
    while sh < nseg:
        x_sh = jnp.where(sub >= sh, pltpu.roll(x_end, sh, 1), 0.0)
        p_sh = jnp.where(sub >= sh, pltpu.roll(p_end, sh, 1), 1.0)
        x_end = p_end * x_sh + x_end
        p_end = p_end * p_sh
        sh *= 2
    carry = carry_sc[...]
    leave = x_end + p_end * carry
    h_init = jnp.where(sub >= 1, pltpu.roll(leave, 1, 1), carry)
    carry_sc[...] = jnp.broadcast_to(leave[:, nseg - 1:nseg, :], carry_sc.shape)

    for c in range(nsl):
        for i in range(nseg):
            src = pl.ds(i, seg, stride=nseg)
            h = tb_sc[c, src, :] + ta_sc[c, src, :] * h_init[c, i:i + 1, :]
            g = gate[i * seg:(i + 1) * seg, c * LANES:(c + 1) * LANES].astype(F32)
            hg_ref[i * seg:(i + 1) * seg, c * LANES:(c + 1) * LANES] = (h * g).astype(BF16)


def _inproj(x, mod, w_in_b, conv_w, conv_b, wg, ba, bx, lam):
    B, S, D = x.shape
    tm = TM_IN
    W = LRU_WIDTH
    ncols = w_in_b.shape[1]
    row = lambda b, i: (b, i, 0)
    const2 = lambda b, i: (0, 0)
    vec = pl.BlockSpec((1, W), const2)
    bf_shape = jax.ShapeDtypeStruct((B, S, D), BF16)
    row_spec = pl.BlockSpec((None, tm, D), row)
    slabs = pltpu.VMEM((W // LANES, tm, LANES), F32)
    qkv_shapes, qkv_specs = [], []
    for _, dil in DILATED_GROUPS:
        if dil == 1:
            qkv_shapes.append(jax.ShapeDtypeStruct((3, B, S, GROUP_W), BF16))
            qkv_specs.append(pl.BlockSpec((3, None, tm, GROUP_W), lambda b, i: (0, b, i, 0)))
        else:
            qkv_shapes.append(jax.ShapeDtypeStruct((3, B, dil, S // dil, GROUP_W), BF16))
            qkv_specs.append(pl.BlockSpec((3, None, dil, tm // dil, GROUP_W), lambda b, i: (0, b, 0, i, 0)))
    return pl.pallas_call(
        _inproj_kernel,
        grid=(B, S // tm),
        in_specs=[row_spec,
                  pl.BlockSpec((None, 6, D), lambda b, i: (b, 0, 0)),
                  _resident((D, ncols), const2),
                  pl.BlockSpec((LRU_CONV, W), const2), vec,
                  _resident(wg.shape, lambda b, i: (0, 0, 0)), vec, vec, vec],
        out_specs=qkv_specs + [row_spec, row_spec, row_spec],
        out_shape=qkv_shapes + [bf_shape, bf_shape, bf_shape],
        scratch_shapes=[pltpu.VMEM((2, SLABS, tm, LANES), F32),
                        pltpu.VMEM((tm + HALO, W), F32), pltpu.VMEM((W // LANES, SUBLANES, LANES), F32),
                        slabs, slabs],
        compiler_params=_cparams(("parallel", "arbitrary")),
        name="inproj",
    )(x, mod, w_in_b, conv_w, conv_b.reshape(1, W), wg, ba.reshape(1, W), bx.reshape(1, W), lam.reshape(1, W))


def _bias_kernel(tab_ref, bucket_ref, o_ref):
    hh = pl.program_id(0)
    bucket = bucket_ref[...]
    acc = jnp.zeros(bucket.shape, F32)
    for b in range(NUM_BUCKETS):
        acc = jnp.where(bucket == b, tab_ref[b, hh], acc)
    qi = lax.broadcasted_iota(jnp.int32, bucket.shape, 0)
    kj = lax.broadcasted_iota(jnp.int32, bucket.shape, 1)
    dist = qi + BLOCK - kj
    band = (dist >= 0) & (dist <= BLOCK)
    o_ref[...] = jnp.where(band, acc * LOG2E, -jnp.inf)


def _bias_tiles(rel_bias, buckets):
    nh = rel_bias.shape[1]
    return pl.pallas_call(
        _bias_kernel,
        grid=(nh,),
        in_specs=[pl.BlockSpec(memory_space=pltpu.SMEM),
                  pl.BlockSpec((None, BLOCK, 2 * BLOCK), lambda h: (h // HEADS_PER_GROUP, 0, 0))],
        out_specs=pl.BlockSpec((None, BLOCK, 2 * BLOCK), lambda h: (h, 0, 0)),
        out_shape=jax.ShapeDtypeStruct((nh, BLOCK, 2 * BLOCK), F32),
        compiler_params=_cparams(("arbitrary",)),
        name="bias_tiles",
    )(rel_bias, buckets)


def _bucket_maps():
    qi = jnp.arange(BLOCK)[:, None]
    kj = jnp.arange(2 * BLOCK)[None, :]
    dist = jnp.maximum(qi + BLOCK - kj, 0)
    maps = []
    for _, dil in DILATED_GROUPS:
        d = dist * dil
        nf = jnp.maximum(d, 1).astype(F32)
        large = MAX_EXACT + (jnp.log(nf / MAX_EXACT) / math.log(MAX_DISTANCE / MAX_EXACT)
                             * (NUM_BUCKETS - MAX_EXACT)).astype(jnp.int32)
        large = jnp.minimum(large, NUM_BUCKETS - 1)
        maps.append(jnp.where(d < MAX_EXACT, d, large))
    return jnp.stack(maps, 0).astype(jnp.int32)


def _attn_kernel(g0_ref, g1_ref, g2_ref, bias_ref, o_ref, part_sc):
    hp = pl.program_id(1)
    lane = lax.broadcasted_iota(jnp.int32, (BLOCK, PAIR_W), 1)
    first_head = lane < HEAD_DIM

    def core(g, qp, kp, vp, first):
        zero = jnp.zeros_like(qp)
        bias2 = bias_ref[g * N_PAIRS + hp]
        if first:
            bias2 = bias2[:, BLOCK:]
        v1 = jnp.concatenate([vp, jnp.ones(vp.shape, BF16)], axis=1)
        ms, pvs = [], []
        for e in range(2):
            qe = jnp.where(first_head, qp, zero) if e == 0 else jnp.where(first_head, zero, qp)
            logits = (lax.dot_general(qe, kp, (((1,), (1,)), ((), ())), preferred_element_type=F32)
                      + bias2[e * BLOCK:(e + 1) * BLOCK])
            m = jnp.max(logits, axis=-1, keepdims=True)
            p = jnp.exp2(logits - m).astype(BF16)
            ms.append(m)
            pvs.append(jnp.dot(p, v1, preferred_element_type=F32))
        pair = lambda a, b: jnp.where(first_head, a, b)
        return (pair(ms[0], ms[1]), pair(pvs[0][:, PAIR_W:], pvs[1][:, PAIR_W:]),
                pair(pvs[0][:, :PAIR_W], pvs[1][:, :PAIR_W]))

    def qkv_block(rows_ref, r0, first):
        qp = rows_ref(0)[pl.ds(r0, BLOCK), :]
        if first:
            return qp, rows_ref(1)[pl.ds(r0, BLOCK), :], rows_ref(2)[pl.ds(r0, BLOCK), :]
        k0 = r0 - BLOCK if isinstance(r0, int) else pl.multiple_of(r0 - BLOCK, BLOCK)
        return qp, rows_ref(1)[pl.ds(k0, 2 * BLOCK), :], rows_ref(2)[pl.ds(k0, 2 * BLOCK), :]

    def dilated_group(g, ref):
        dil = DILATED_GROUPS[g][1]
        nblk = ref.shape[2] // BLOCK

        def body(r, carry):
            for n in range(nblk):
                qp, kp, vp = qkv_block(lambda w: ref.at[w, r], n * BLOCK, n == 0)
                parts = core(g, qp, kp, vp, n == 0)
                for j, t in enumerate(parts):
                    part_sc[3 * (g - 1) + j, pl.ds(n * BLOCK * dil + r, BLOCK, stride=dil), :] = t
            return carry
        lax.fori_loop(0, dil, body, 0, unroll=ATTN_UNROLL[g])

    dilated_group(1, g1_ref)
    dilated_group(2, g2_ref)

    def dense_block(r0, first):
        qp, kp, vp = qkv_block(lambda w: g0_ref.at[w], r0, first)
        m0, s0, pv0 = core(0, qp, kp, vp, first)
        rows = pl.ds(r0, BLOCK)
        m1, s1, pv1 = part_sc[0, rows, :], part_sc[1, rows, :], part_sc[2, rows, :]
        m2, s2, pv2 = part_sc[3, rows, :], part_sc[4, rows, :], part_sc[5, rows, :]
        mx = jnp.maximum(jnp.maximum(m0, m1), m2)
        a0, a1, a2 = jnp.exp2(m0 - mx), jnp.exp2(m1 - mx), jnp.exp2(m2 - mx)
        den = a0 * s0 + a1 * s1 + a2 * s2
        num = a0 * pv0 + a1 * pv1 + a2 * pv2
        o_ref[rows, :] = (num / den).astype(o_ref.dtype)

    dense_block(0, True)

    def body0(n, carry):
        dense_block(pl.multiple_of(n * BLOCK, BLOCK), False)
        return carry
    lax.fori_loop(1, g0_ref.shape[1] // BLOCK, body0, 0, unroll=ATTN_UNROLL[0])


def _attention(qkv0, qkv1, qkv2, bias):
    _, B, S, _ = qkv0.shape
    specs = [pl.BlockSpec((3, None, S, PAIR_W), lambda b, h: (0, b, 0, h))]
    for t in (qkv1, qkv2):
        dil, L = t.shape[2], t.shape[3]
        specs.append(pl.BlockSpec((3, None, dil, L, PAIR_W), lambda b, h: (0, b, 0, 0, h)))
    bias2 = bias.reshape(N_GROUPS * N_PAIRS, 2 * BLOCK, 2 * BLOCK)
    specs.append(_resident(bias2.shape, lambda b, h: (0, 0, 0)))
    return pl.pallas_call(
        _attn_kernel,
        grid=(B, N_PAIRS),
        in_specs=specs,
        out_specs=pl.BlockSpec((None, S, PAIR_W), lambda b, h: (b, 0, h)),
        out_shape=jax.ShapeDtypeStruct((B, S, GROUP_W), BF16),
        scratch_shapes=[pltpu.VMEM((3 * (N_GROUPS - 1), S, LANES), F32)],
        compiler_params=_cparams(("parallel", "parallel")),
        name="attn",
    )(qkv0, qkv1, qkv2, bias2)


def _lru_gate_weights(wa, wx):
    per = LRU_GRP // LRU_BLOCK_W
    ngrp = LRU_BLOCKS // per

    def bd(w):
        w = w.reshape(ngrp, per, LRU_BLOCK_W, LRU_BLOCK_W)
        eye = jnp.eye(per, dtype=w.dtype)
        t = jnp.einsum('gpcd,pq->gpcqd', w, eye)
        return t.reshape(ngrp, LRU_GRP, LRU_GRP)

    return jnp.concatenate([bd(wa), bd(wx)], axis=2).astype(BF16)


def _mix_kernel(x_ref, oa_ref, hg_ref, ga_ref, gr_ref, mod_ref, wpa_ref, wpl_ref, wo_ref,
                g_ref, b_ref, o_ref):
    g1 = mod_ref[2:3, :]
    tm = x_ref.shape[0]
    sub = tm // MIX_SPLIT
    for h in range(MIX_SPLIT):
        rows = slice(h * sub, (h + 1) * sub)
        ya = jnp.dot(oa_ref[rows, :], wpa_ref[...], preferred_element_type=F32)
        yl = jnp.dot(hg_ref[rows, :], wpl_ref[...], preferred_element_type=F32)
        merged = ga_ref[rows, :].astype(F32) * ya + gr_ref[rows, :].astype(F32) * yl
        mo = jnp.dot(merged.astype(BF16), wo_ref[...], preferred_element_type=F32)
        z = ALPHA * x_ref[rows, :] + g1 * mo
        o_ref[rows, :] = _layer_norm(z, g_ref[...], b_ref[...])


def _mix(x, o_attn, hg, ga, gr, mod, wpa, wpl, wo, ln_g, ln_b):
    B, S, D = x.shape
    tm = TM_MIX
    row = lambda b, i: (b, i, 0)
    const2 = lambda b, i: (0, 0)
    rspec = lambda w: pl.BlockSpec((None, tm, w), row)
    vec = pl.BlockSpec((1, D), const2)
    return pl.pallas_call(
        _mix_kernel,
        grid=(B, S // tm),
        in_specs=[rspec(D), rspec(GROUP_W), rspec(LRU_WIDTH), rspec(D), rspec(D),
                  pl.BlockSpec((None, 6, D), lambda b, i: (b, 0, 0)),
                  _resident(wpa.shape, const2), _resident(wpl.shape, const2), _resident(wo.shape, const2),
                  vec, vec],
        out_specs=rspec(D),
        out_shape=jax.ShapeDtypeStruct((B, S, D), F32),
        compiler_params=_cparams(("parallel", "parallel")),
        name="mix",
    )(x, o_attn, hg, ga, gr, mod, wpa, wpl, wo, ln_g.reshape(1, D), ln_b.reshape(1, D))


def _ffn_kernel(x_ref, mod_ref, wup_ref, cw_ref, cb_ref, wdn_ref, g_ref, b_ref, o_ref, perm_sc, carry_sc):
    tm = x_ref.shape[0]
    nseg = SUBLANES
    seg = tm // nseg
    nsl = D_MODEL // LANES
    back = FFN_CONV - 1

    @pl.when(pl.program_id(1) == 0)
    def _():
        carry_sc[...] = jnp.zeros(carry_sc.shape, F32)

    for c in range(nsl):
        for i in range(nseg):
            perm_sc[c, pl.ds(i, seg, stride=nseg), :] = x_ref[i * seg:(i + 1) * seg, c * LANES:(c + 1) * LANES]
    x1 = jnp.concatenate([perm_sc[c] for c in range(nsl)], axis=1)
    sh2 = mod_ref[3:4, :]
    sc2 = mod_ref[4:5, :]
    g2 = mod_ref[5:6, :]
    u2 = (x1 * (1.0 + sc2) + sh2).astype(BF16)

    def conv_up(c0, c1):
        h = jnp.dot(u2, wup_ref[:, c0:c1], preferred_element_type=F32)
        sub = lax.broadcasted_iota(jnp.int32, (nseg, c1 - c0), 0)
        head = []
        for v in range(back):
            rows = slice(tm - (back - v) * nseg, tm - (back - v - 1) * nseg)
            prev = carry_sc[v * nseg:(v + 1) * nseg, c0:c1]
            head.append(jnp.where(sub >= 1, pltpu.roll(h[rows, :], 1, 0), pltpu.roll(prev, 1, 0)))
            carry_sc[v * nseg:(v + 1) * nseg, c0:c1] = h[rows, :]
        hx = jnp.concatenate(head + [h], axis=0)
        y = cb_ref[:, c0:c1] + cw_ref[back:back + 1, c0:c1] * h
        for kk in range(back):
            y = y + cw_ref[kk:kk + 1, c0:c1] * hx[kk * nseg:kk * nseg + tm, :]
        return y

    acc = jnp.zeros((tm, D_MODEL), F32)
    c0 = 0
    for width in FF_CHUNKS:
        val = conv_up(c0, c0 + width)
        gate = conv_up(D_FF + c0, D_FF + c0 + width)
        act = (_gelu_tanh(gate) * val).astype(BF16)
        acc = acc + jnp.dot(act, wdn_ref[c0:c0 + width, :], preferred_element_type=F32)
        c0 += width
    z = ALPHA * x1 + g2 * acc
    out = _layer_norm(z, g_ref[...], b_ref[...])
    for c in range(nsl):
        perm_sc[c] = out[:, c * LANES:(c + 1) * LANES]
    for c in range(nsl):
        for i in range(nseg):
            o_ref[i * seg:(i + 1) * seg, c * LANES:(c + 1) * LANES] = perm_sc[c, pl.ds(i, seg, stride=nseg), :]


def _ffn(x1, mod, wup, cw, cb, wdn, ln_g, ln_b):
    B, S, D = x1.shape
    tm = TM_FFN
    row = lambda b, i: (b, i, 0)
    const2 = lambda b, i: (0, 0)
    vec = pl.BlockSpec((1, D), const2)
    return pl.pallas_call(
        _ffn_kernel,
        grid=(B, S // tm),
        in_specs=[pl.BlockSpec((None, tm, D), row),
                  pl.BlockSpec((None, 6, D), lambda b, i: (b, 0, 0)),
                  _resident(wup.shape, const2), _resident(cw.shape, const2),
                  _resident(cb.shape, const2), _resident(wdn.shape, const2),
                  vec, vec],
        out_specs=pl.BlockSpec((None, tm, D), row),
        out_shape=jax.ShapeDtypeStruct((B, S, D), F32),
        scratch_shapes=[pltpu.VMEM((D // LANES, tm, LANES), F32),
                        pltpu.VMEM(((FFN_CONV - 1) * SUBLANES, 2 * D_FF), F32)],
        compiler_params=_cparams(("parallel", "arbitrary")),
        name="ffn",
    )(x1, mod, wup, cw, cb, wdn, ln_g.reshape(1, D), ln_b.reshape(1, D))


def kernel(x, c, w_ada, b_ada, w_in, rel_bias, lru_conv_w, lru_conv_b, lru_wa, lru_ba, lru_wx, lru_bx,
           lru_lambda, w_proj_attn, w_proj_lru, w_out, ln1_g, ln1_b, ffn_w_up, ffn_conv_w, ffn_conv_b,
           ffn_w_down, ln2_g, ln2_b):
    B, S, D = x.shape
    assert w_ada.shape[0] == DEPTH == 1 and D == D_MODEL
    assert S == DILATED_GROUPS[-1][1] * BLOCK and S % TM_IN == 0
    assert all(window // dil == BLOCK for window, dil in DILATED_GROUPS)
    l = 0
    mod = _ada(c, w_ada[l], b_ada[l]).reshape(B, 6, D)

    wg = _lru_gate_weights(lru_wa[l], lru_wx[l])
    qkv0, qkv1, qkv2, hg, ga, gr = _inproj(x, mod, w_in[l].astype(BF16), lru_conv_w[l], lru_conv_b[l], wg,
                                           lru_ba[l], lru_bx[l], lru_lambda[l])

    bias = _bias_tiles(rel_bias, _bucket_maps())
    o_attn = _attention(qkv0, qkv1, qkv2, bias)

    x1 = _mix(x, o_attn, hg, ga, gr, mod, w_proj_attn[l].astype(BF16), w_proj_lru[l].astype(BF16),
              w_out[l].astype(BF16), ln1_g[l], ln1_b[l])

    return _ffn(x1, mod, ffn_w_up[l].astype(BF16), ffn_conv_w[l], ffn_conv_b[l][None, :],
                ffn_w_down[l].astype(BF16), ln2_g[l], ln2_b[l])
```

```python
import functools
import math

import jax
import jax.numpy as jnp
from jax import lax
from jax.experimental import pallas as pl
from jax.experimental.pallas import tpu as pltpu

F32 = jnp.float32
BF16 = jnp.bfloat16

D_MODEL = 1024
HEAD_DIM = 64
HEADS_PER_GROUP = 8
DILATED_GROUPS = ((128, 1), (512, 4), (2048, 16))
N_GROUPS = len(DILATED_GROUPS)
GROUP_W = HEADS_PER_GROUP * HEAD_DIM
ATTN_QKV = N_GROUPS * GROUP_W
BLOCK = 128
LOG2E = math.log2(math.e)
NUM_BUCKETS = 32
MAX_EXACT = NUM_BUCKETS // 2
MAX_DISTANCE = 2048
LRU_WIDTH = D_MODEL
LRU_BLOCKS = 16
LRU_BLOCK_W = LRU_WIDTH // LRU_BLOCKS
LRU_CONV = 4
C_RGLRU = 8.0
D_FF = 3 * D_MODEL
FFN_CONV = 3
DEPTH = 1
ALPHA = (2.0 * DEPTH) ** 0.25
LN_EPS = 1e-5

LANES = 128
SUBLANES = 8
VMEM_LIMIT = 56 * 1024 * 1024
TM_IN = 512
TM_MIX = 512
MIX_SPLIT = 2
TM_FFN = 512
LRU_SCAN_UNROLL = 4
FF_CHUNKS = (1536, 1536)
assert sum(FF_CHUNKS) == D_FF
LRU_GRP = 256
HALO = SUBLANES
PAIR_W = 2 * HEAD_DIM
N_PAIRS = GROUP_W // PAIR_W
SLABS = GROUP_W // LANES


def _cparams(sem):
    return pltpu.CompilerParams(dimension_semantics=sem, vmem_limit_bytes=VMEM_LIMIT)


def _resident(shape, index_map):
    return pl.BlockSpec(shape, index_map, pipeline_mode=pl.Buffered(1))


def _gelu_tanh(x):
    return 0.5 * x * (1.0 + jnp.tanh(math.sqrt(2.0 / math.pi) * (x + 0.044715 * (x * x * x))))


def _sigmoid(x):
    return 0.5 * jnp.tanh(0.5 * x) + 0.5


def _layer_norm(z, g, b):
    mu = jnp.mean(z, axis=-1, keepdims=True)
    zc = z - mu
    var = jnp.mean(zc * zc, axis=-1, keepdims=True)
    return zc * lax.rsqrt(var + LN_EPS) * g + b


def _ada_kernel(c_ref, w_ref, b_ref, o_ref):
    c = c_ref[...]
    ca = (c * _sigmoid(c)).astype(BF16)
    o_ref[...] = jnp.dot(ca, w_ref[...].astype(BF16), preferred_element_type=F32) + b_ref[...]


def _ada(c, w_ada, b_ada):
    B, D = c.shape
    N = w_ada.shape[1]
    tn = D
    return pl.pallas_call(
        _ada_kernel,
        grid=(N // tn,),
        in_specs=[pl.BlockSpec((B, D), lambda j: (0, 0)),
                  pl.BlockSpec((D, tn), lambda j: (0, j)),
                  pl.BlockSpec((1, tn), lambda j: (0, j))],
        out_specs=pl.BlockSpec((B, tn), lambda j: (0, j)),
        out_shape=jax.ShapeDtypeStruct((B, N), F32),
        compiler_params=_cparams(("arbitrary",)),
        name="ada",
    )(c, w_ada, b_ada.reshape(1, N))


def _inproj_kernel(x_ref, mod_ref, w_ref, cw_ref, cb_ref, wg_ref, ba_ref, bx_ref, lam_ref,
                   g0_ref, g1_ref, g2_ref, hg_ref, ga_ref, gr_ref,
                   split_sc, xpad_sc, carry_sc, ta_sc, tb_sc):
    tm = x_ref.shape[0]
    W = LRU_WIDTH
    nseg = SUBLANES
    seg = tm // nseg
    nsl = W // LANES
    sh1 = mod_ref[0:1, :]
    sc1 = mod_ref[1:2, :]
    u = (x_ref[...] * (1.0 + sc1) + sh1).astype(BF16)

    def mm(c0, n):
        return jnp.dot(u, w_ref[:, c0:c0 + n], preferred_element_type=F32)

    @pl.when(pl.program_id(1) == 0)
    def _():
        xpad_sc[0:HALO, :] = jnp.zeros((HALO, W), F32)
        carry_sc[...] = jnp.zeros(carry_sc.shape, F32)

    col_lru = 3 * ATTN_QKV
    xl = mm(col_lru, W)
    xpad_sc[HALO:HALO + tm, :] = xl
    xp = xpad_sc[...]
    t = cw_ref[0:1, :] * xp
    for kk in range(1, LRU_CONV):
        t = cw_ref[kk:kk + 1, :] * xp + pltpu.roll(t, 1, 0)
    xr = t[HALO:, :] + cb_ref[...]
    xpad_sc[0:HALO, :] = xl[tm - HALO:tm, :]

    n_split = [0]

    def store_group(g, which, res):
        ref = (g0_ref, g1_ref, g2_ref)[g]
        dil = DILATED_GROUPS[g][1]
        if dil == 1:
            ref[which] = res.astype(BF16)
            return
        buf = n_split[0] % split_sc.shape[0]
        n_split[0] += 1
        for s in range(SLABS):
            split_sc[buf, s] = res[:, s * LANES:(s + 1) * LANES]
        rows = tm // dil
        for r in range(dil):
            for s in range(SLABS):
                ref[which, r, :, s * LANES:(s + 1) * LANES] = (
                    split_sc[buf, s, pl.ds(r, rows, stride=dil), :].astype(BF16))

    def qkv(which):
        res = mm(which * ATTN_QKV, ATTN_QKV)
        if which == 0:
            res = res * (HEAD_DIM ** -0.5 * LOG2E)
        for g in range(N_GROUPS):
            store_group(g, which, res[:, g * GROUP_W:(g + 1) * GROUP_W])

    qkv(0)
    xb = xr.astype(BF16)
    zs = [jnp.dot(xb[:, c * LRU_GRP:(c + 1) * LRU_GRP], wg_ref[c], preferred_element_type=F32)
          for c in range(W // LRU_GRP)]
    za = jnp.concatenate([z[:, :LRU_GRP] for z in zs], axis=1)
    zx = jnp.concatenate([z[:, LRU_GRP:] for z in zs], axis=1)
    gi = _sigmoid(zx + bx_ref[...])
    nl = -lam_ref[...]
    softplus = jnp.maximum(nl, 0.0) + jnp.log1p(jnp.exp(-jnp.abs(nl)))
    half = (0.5 * C_RGLRU) * softplus
    neg_log_a = half * jnp.tanh(0.5 * (za + ba_ref[...])) + half
    a = jnp.exp2(neg_log_a * (-LOG2E))
    gap = jnp.tanh(neg_log_a) * (a * a + 1.0)
    root = jnp.where(gap > 0.0, gap * lax.rsqrt(gap), 0.0)
    bt = root * (gi * xr)
    for c in range(nsl):
        for i in range(nseg):
            dst = pl.ds(i, seg, stride=nseg)
            ta_sc[c, dst, :] = a[i * seg:(i + 1) * seg, c * LANES:(c + 1) * LANES]
            tb_sc[c, dst, :] = bt[i * seg:(i + 1) * seg, c * LANES:(c + 1) * LANES]
    qkv(1)
    qkv(2)

    gate = _gelu_tanh(mm(col_lru + W, W)).astype(BF16)
    col = col_lru + 2 * W
    for ref in (ga_ref, gr_ref):
        ref[...] = _sigmoid(mm(col, D_MODEL)).astype(BF16)
        col += D_MODEL

    def step(j, hp):
        h, p = hp
        rows = pl.ds(pl.multiple_of(j * nseg, nseg), nseg)
        aj = ta_sc[:, rows, :]
        h = aj * h + tb_sc[:, rows, :]
        p = aj * p
        tb_sc[:, rows, :] = h
        ta_sc[:, rows, :] = p
        return h, p

    state = (jnp.zeros((nsl, nseg, LANES), F32), jnp.ones((nsl, nseg, LANES), F32))
    x_end, p_end = lax.fori_loop(0, seg, step, state, unroll=LRU_SCAN_UNROLL)

    sub = lax.broadcasted_iota(jnp.int32, (nsl, nseg, LANES), 1)
    sh = 1
    while sh < nseg:
        x_sh = jnp.where(sub >= sh, pltpu.roll(x_end, sh, 1), 0.0)
        p_sh = jnp.where(sub >= sh, pltpu.roll(p_end, sh, 1), 1.0)
        x_end = p_end * x_sh + x_end
        p_end = p_end * p_sh
        sh *= 2
    carry = carry_sc[...]
    leave = x_end + p_end * carry
    h_init = jnp.where(sub >= 1, pltpu.roll(leave, 1, 1), carry)
    carry_sc[...] = jnp.broadcast_to(leave[:, nseg - 1:nseg, :], carry_sc.shape)

    for c in range(nsl):
        for i in range(nseg):
            src = pl.ds(i, seg, stride=nseg)
            h = tb_sc[c, src, :] + ta_sc[c, src, :] * h_init[c, i:i + 1, :]
            g = gate[i * seg:(i + 1) * seg, c * LANES:(c + 1) * LANES].astype(F32)
            hg_ref[i * seg:(i + 1) * seg, c * LANES:(c + 1) * LANES] = (h * g).astype(BF16)


def _inproj(x, mod, w_in_b, conv_w, conv_b, wg, ba, bx, lam):
    B, S, D = x.shape
    tm = TM_IN
    W = LRU_WIDTH
    ncols = w_in_b.shape[1]
    row = lambda b, i: (b, i, 0)
    const2 = lambda b, i: (0, 0)
    vec = pl.BlockSpec((1, W), const2)
    bf_shape = jax.ShapeDtypeStruct((B, S, D), BF16)
    row_spec = pl.BlockSpec((None, tm, D), row)
    qkv_shapes, qkv_specs = [], []
    for _, dil in DILATED_GROUPS:
        if dil == 1:
            qkv_shapes.append(jax.ShapeDtypeStruct((3, B, S, GROUP_W), BF16))
            qkv_specs.append(pl.BlockSpec((3, None, tm, GROUP_W), lambda b, i: (0, b, i, 0)))
        else:
            qkv_shapes.append(jax.ShapeDtypeStruct((3, B, dil, S // dil, GROUP_W), BF16))
            qkv_specs.append(pl.BlockSpec((3, None, dil, tm // dil, GROUP_W), lambda b, i: (0, b, 0, i, 0)))
    return pl.pallas_call(
        _inproj_kernel,
        grid=(B, S // tm),
        in_specs=[row_spec,
                  pl.BlockSpec((None, 6, D), lambda b, i: (b, 0, 0)),
                  _resident((D, ncols), const2),
                  pl.BlockSpec((LRU_CONV, W), const2), vec,
                  _resident(wg.shape, lambda b, i: (0, 0, 0)), vec, vec, vec],
        out_specs=qkv_specs + [row_spec, row_spec, row_spec],
        out_shape=qkv_shapes + [bf_shape, bf_shape, bf_shape],
        scratch_shapes=[pltpu.VMEM((2, SLABS, tm, LANES), F32),
                        pltpu.VMEM((tm + HALO, W), F32), pltpu.VMEM((W // LANES, SUBLANES, LANES), F32),
                        pltpu.VMEM((W // LANES, tm, LANES), F32), pltpu.VMEM((W // LANES, tm, LANES), F32)],
        compiler_params=_cparams(("parallel", "arbitrary")),
        name="inproj",
    )(x, mod, w_in_b, conv_w, conv_b.reshape(1, W), wg, ba.reshape(1, W), bx.reshape(1, W), lam.reshape(1, W))


def _bias_kernel(tab_ref, bucket_ref, o_ref):
    hh = pl.program_id(0)
    bucket = bucket_ref[...]
    acc = jnp.zeros(bucket.shape, F32)
    for b in range(NUM_BUCKETS):
        acc = jnp.where(bucket == b, tab_ref[b, hh], acc)
    qi = lax.broadcasted_iota(jnp.int32, bucket.shape, 0)
    kj = lax.broadcasted_iota(jnp.int32, bucket.shape, 1)
    dist = qi + BLOCK - kj
    band = (dist >= 0) & (dist <= BLOCK)
    o_ref[...] = jnp.where(band, acc * LOG2E, -jnp.inf)


def _bias_tiles(rel_bias, buckets):
    nh = rel_bias.shape[1]
    return pl.pallas_call(
        _bias_kernel,
        grid=(nh,),
        in_specs=[pl.BlockSpec(memory_space=pltpu.SMEM),
                  pl.BlockSpec((None, BLOCK, 2 * BLOCK), lambda h: (h // HEADS_PER_GROUP, 0, 0))],
        out_specs=pl.BlockSpec((None, BLOCK, 2 * BLOCK), lambda h: (h, 0, 0)),
        out_shape=jax.ShapeDtypeStruct((nh, BLOCK, 2 * BLOCK), F32),
        compiler_params=_cparams(("arbitrary",)),
        name="bias_tiles",
    )(rel_bias, buckets)


def _bucket_maps():
    qi = jnp.arange(BLOCK)[:, None]
    kj = jnp.arange(2 * BLOCK)[None, :]
    dist = jnp.maximum(qi + BLOCK - kj, 0)
    maps = []
    for _, dil in DILATED_GROUPS:
        d = dist * dil
        nf = jnp.maximum(d, 1).astype(F32)
        large = MAX_EXACT + (jnp.log(nf / MAX_EXACT) / math.log(MAX_DISTANCE / MAX_EXACT)
                             * (NUM_BUCKETS - MAX_EXACT)).astype(jnp.int32)
        large = jnp.minimum(large, NUM_BUCKETS - 1)
        maps.append(jnp.where(d < MAX_EXACT, d, large))
    return jnp.stack(maps, 0).astype(jnp.int32)


def _attn_kernel(g0_ref, g1_ref, g2_ref, bias_ref, o_ref, part_sc):
    hp = pl.program_id(1)
    lane = lax.broadcasted_iota(jnp.int32, (BLOCK, PAIR_W), 1)
    first_head = lane < HEAD_DIM

    refs = (g0_ref, g1_ref, g2_ref)
    blocks = [(g, r, n) for g in (1, 2) for r in range(DILATED_GROUPS[g][1])
              for n in range(refs[g].shape[2] // BLOCK)]
    blocks += [(0, 0, n) for n in range(g0_ref.shape[1] // BLOCK)]

    def rows_of(g, r, w):
        return refs[g].at[w] if g == 0 else refs[g].at[w, r]

    def logits_of(g, r, n, e):
        qp = rows_of(g, r, 0)[n * BLOCK:(n + 1) * BLOCK, :]
        k0 = max(n - 1, 0) * BLOCK
        kp = rows_of(g, r, 1)[k0:(n + 1) * BLOCK, :]
        zero = jnp.zeros_like(qp)
        qe = jnp.where(first_head, qp, zero) if e == 0 else jnp.where(first_head, zero, qp)
        bias = bias_ref[g * N_PAIRS + hp, e * BLOCK:(e + 1) * BLOCK, :]
        if n == 0:
            bias = bias[:, BLOCK:]
        return lax.dot_general(qe, kp, (((1,), (1,)), ((), ())), preferred_element_type=F32) + bias

    def softmax_of(logits):
        m = jnp.max(logits, axis=-1, keepdims=True)
        return m, jnp.exp2(logits - m).astype(BF16)

    def weighted_values(g, r, n, p):
        vp = rows_of(g, r, 2)[max(n - 1, 0) * BLOCK:(n + 1) * BLOCK, :]
        v1 = jnp.concatenate([vp, jnp.ones(vp.shape, BF16)], axis=1)
        return jnp.dot(p, v1, preferred_element_type=F32)

    def finish(g, r, n, heads):
        (ma, pva), (mb, pvb) = heads
        pair = lambda a, b: jnp.where(first_head, a, b)
        m0, s0, pv0 = pair(ma, mb), pair(pva[:, PAIR_W:], pvb[:, PAIR_W:]), pair(pva[:, :PAIR_W], pvb[:, :PAIR_W])
        if g > 0:
            dil = DILATED_GROUPS[g][1]
            for j, t in enumerate((m0, s0, pv0)):
                part_sc[3 * (g - 1) + j, pl.ds(n * BLOCK * dil + r, BLOCK, stride=dil), :] = t
            return
        rows = slice(n * BLOCK, (n + 1) * BLOCK)
        m1, s1, pv1 = part_sc[0, rows, :], part_sc[1, rows, :], part_sc[2, rows, :]
        m2, s2, pv2 = part_sc[3, rows, :], part_sc[4, rows, :], part_sc[5, rows, :]
        mx = jnp.maximum(jnp.maximum(m0, m1), m2)
        a0, a1, a2 = jnp.exp2(m0 - mx), jnp.exp2(m1 - mx), jnp.exp2(m2 - mx)
        den = a0 * s0 + a1 * s1 + a2 * s2
        num = a0 * pv0 + a1 * pv1 + a2 * pv2
        o_ref[rows, :] = (num / den).astype(o_ref.dtype)

    units = [(blk, e) for blk in blocks for e in range(2)]
    logits = logits_of(*units[0][0], units[0][1])
    heads = []
    for i, (blk, e) in enumerate(units):
        nxt = logits_of(*units[i + 1][0], units[i + 1][1]) if i + 1 < len(units) else None
        m, p = softmax_of(logits)
        heads.append((m, weighted_values(*blk, p)))
        if e == 1:
            finish(*blk, heads)
            heads = []
        logits = nxt


def _attention(qkv0, qkv1, qkv2, bias):
    _, B, S, _ = qkv0.shape
    specs = [pl.BlockSpec((3, None, S, PAIR_W), lambda b, h: (0, b, 0, h))]
    for t in (qkv1, qkv2):
        dil, L = t.shape[2], t.shape[3]
        specs.append(pl.BlockSpec((3, None, dil, L, PAIR_W), lambda b, h: (0, b, 0, 0, h)))
    bias2 = bias.reshape(N_GROUPS * N_PAIRS, 2 * BLOCK, 2 * BLOCK)
    specs.append(_resident(bias2.shape, lambda b, h: (0, 0, 0)))
    return pl.pallas_call(
        _attn_kernel,
        grid=(B, N_PAIRS),
        in_specs=specs,
        out_specs=pl.BlockSpec((None, S, PAIR_W), lambda b, h: (b, 0, h)),
        out_shape=jax.ShapeDtypeStruct((B, S, GROUP_W), BF16),
        scratch_shapes=[pltpu.VMEM((3 * (N_GROUPS - 1), S, LANES), F32)],
        compiler_params=_cparams(("parallel", "parallel")),
        name="attn",
    )(qkv0, qkv1, qkv2, bias2)


def _lru_gate_weights(wa, wx):
    per = LRU_GRP // LRU_BLOCK_W
    ngrp = LRU_BLOCKS // per

    def bd(w):
        w = w.reshape(ngrp, per, LRU_BLOCK_W, LRU_BLOCK_W)
        eye = jnp.eye(per, dtype=w.dtype)
        t = jnp.einsum('gpcd,pq->gpcqd', w, eye)
        return t.reshape(ngrp, LRU_GRP, LRU_GRP)

    return jnp.concatenate([bd(wa), bd(wx)], axis=2).astype(BF16)


def _mix_kernel(x_ref, oa_ref, hg_ref, ga_ref, gr_ref, mod_ref, wpa_ref, wpl_ref, wo_ref,
                g_ref, b_ref, o_ref):
    g1 = mod_ref[2:3, :]
    tm = x_ref.shape[0]
    sub = tm // MIX_SPLIT

    def merged_of(h):
        rows = slice(h * sub, (h + 1) * sub)
        ya = jnp.dot(oa_ref[rows, :], wpa_ref[...], preferred_element_type=F32)
        yl = jnp.dot(hg_ref[rows, :], wpl_ref[...], preferred_element_type=F32)
        return (ga_ref[rows, :].astype(F32) * ya + gr_ref[rows, :].astype(F32) * yl).astype(BF16)

    merged = merged_of(0)
    for h in range(MIX_SPLIT):
        nxt = merged_of(h + 1) if h + 1 < MIX_SPLIT else None
        rows = slice(h * sub, (h + 1) * sub)
        mo = jnp.dot(merged, wo_ref[...], preferred_element_type=F32)
        z = ALPHA * x_ref[rows, :] + g1 * mo
        o_ref[rows, :] = _layer_norm(z, g_ref[...], b_ref[...])
        merged = nxt


def _mix(x, o_attn, hg, ga, gr, mod, wpa, wpl, wo, ln_g, ln_b):
    B, S, D = x.shape
    tm = TM_MIX
    row = lambda b, i: (b, i, 0)
    const2 = lambda b, i: (0, 0)
    rspec = lambda w: pl.BlockSpec((None, tm, w), row)
    vec = pl.BlockSpec((1, D), const2)
    return pl.pallas_call(
        _mix_kernel,
        grid=(B, S // tm),
        in_specs=[rspec(D), rspec(GROUP_W), rspec(LRU_WIDTH), rspec(D), rspec(D),
                  pl.BlockSpec((None, 6, D), lambda b, i: (b, 0, 0)),
                  _resident(wpa.shape, const2), _resident(wpl.shape, const2), _resident(wo.shape, const2),
                  vec, vec],
        out_specs=rspec(D),
        out_shape=jax.ShapeDtypeStruct((B, S, D), F32),
        compiler_params=_cparams(("parallel", "parallel")),
        name="mix",
    )(x, o_attn, hg, ga, gr, mod, wpa, wpl, wo, ln_g.reshape(1, D), ln_b.reshape(1, D))


def _ffn_kernel(x_ref, mod_ref, wup_ref, cw_ref, cb_ref, wdn_ref, g_ref, b_ref, o_ref, perm_sc, carry_sc):
    tm = x_ref.shape[0]
    nseg = SUBLANES
    seg = tm // nseg
    nsl = D_MODEL // LANES
    back = FFN_CONV - 1

    @pl.when(pl.program_id(1) == 0)
    def _():
        carry_sc[...] = jnp.zeros(carry_sc.shape, F32)

    for c in range(nsl):
        for i in range(nseg):
            perm_sc[c, pl.ds(i, seg, stride=nseg), :] = x_ref[i * seg:(i + 1) * seg, c * LANES:(c + 1) * LANES]
    x1 = jnp.concatenate([perm_sc[c] for c in range(nsl)], axis=1)
    sh2 = mod_ref[3:4, :]
    sc2 = mod_ref[4:5, :]
    g2 = mod_ref[5:6, :]
    u2 = (x1 * (1.0 + sc2) + sh2).astype(BF16)

    def up(c0, c1):
        return jnp.dot(u2, wup_ref[:, c0:c1], preferred_element_type=F32)

    def conv(h, c0, c1):
        sub = lax.broadcasted_iota(jnp.int32, (nseg, c1 - c0), 0)
        head = []
        for v in range(back):
            rows = slice(tm - (back - v) * nseg, tm - (back - v - 1) * nseg)
            prev = carry_sc[v * nseg:(v + 1) * nseg, c0:c1]
            head.append(jnp.where(sub >= 1, pltpu.roll(h[rows, :], 1, 0), pltpu.roll(prev, 1, 0)))
            carry_sc[v * nseg:(v + 1) * nseg, c0:c1] = h[rows, :]
        hx = jnp.concatenate(head + [h], axis=0)
        y = cb_ref[:, c0:c1] + cw_ref[back:back + 1, c0:c1] * h
        for kk in range(back):
            y = y + cw_ref[kk:kk + 1, c0:c1] * hx[kk * nseg:kk * nseg + tm, :]
        return y

    bounds = [sum(FF_CHUNKS[:k]) for k in range(len(FF_CHUNKS) + 1)]
    ups = lambda k: (up(bounds[k], bounds[k + 1]), up(D_FF + bounds[k], D_FF + bounds[k + 1]))
    acc = jnp.zeros((tm, D_MODEL), F32)
    hv, hg = ups(0)
    for k in range(len(FF_CHUNKS)):
        nxt = ups(k + 1) if k + 1 < len(FF_CHUNKS) else None
        c0, c1 = bounds[k], bounds[k + 1]
        val = conv(hv, c0, c1)
        gate = conv(hg, D_FF + c0, D_FF + c1)
        act = (_gelu_tanh(gate) * val).astype(BF16)
        acc = acc + jnp.dot(act, wdn_ref[c0:c1, :], preferred_element_type=F32)
        if nxt is not None:
            hv, hg = nxt
    z = ALPHA * x1 + g2 * acc
    out = _layer_norm(z, g_ref[...], b_ref[...])
    for c in range(nsl):
        perm_sc[c] = out[:, c * LANES:(c + 1) * LANES]
    for c in range(nsl):
        for i in range(nseg):
            o_ref[i * seg:(i + 1) * seg, c * LANES:(c + 1) * LANES] = perm_sc[c, pl.ds(i, seg, stride=nseg), :]


def _ffn(x1, mod, wup, cw, cb, wdn, ln_g, ln_b):
    B, S, D = x1.shape
    tm = TM_FFN
    row = lambda b, i: (b, i, 0)
    const2 = lambda b, i: (0, 0)
    vec = pl.BlockSpec((1, D), const2)
    return pl.pallas_call(
        _ffn_kernel,
        grid=(B, S // tm),
        in_specs=[pl.BlockSpec((None, tm, D), row),
                  pl.BlockSpec((None, 6, D), lambda b, i: (b, 0, 0)),
                  _resident(wup.shape, const2), _resident(cw.shape, const2),
                  _resident(cb.shape, const2), _resident(wdn.shape, const2),
                  vec, vec],
        out_specs=pl.BlockSpec((None, tm, D), row),
        out_shape=jax.ShapeDtypeStruct((B, S, D), F32),
        scratch_shapes=[pltpu.VMEM((D // LANES, tm, LANES), F32),
                        pltpu.VMEM(((FFN_CONV - 1) * SUBLANES, 2 * D_FF), F32)],
        compiler_params=_cparams(("parallel", "arbitrary")),
        name="ffn",
    )(x1, mod, wup, cw, cb, wdn, ln_g.reshape(1, D), ln_b.reshape(1, D))


def kernel(x, c, w_ada, b_ada, w_in, rel_bias, lru_conv_w, lru_conv_b, lru_wa, lru_ba, lru_wx, lru_bx,
           lru_lambda, w_proj_attn, w_proj_lru, w_out, ln1_g, ln1_b, ffn_w_up, ffn_conv_w, ffn_conv_b,
           ffn_w_down, ln2_g, ln2_b):
    B, S, D = x.shape
    assert w_ada.shape[0] == DEPTH == 1 and D == D_MODEL
    assert S == DILATED_GROUPS[-1][1] * BLOCK and S % TM_IN == 0
    assert all(window // dil == BLOCK for window, dil in DILATED_GROUPS)
    l = 0
    mod = _ada(c, w_ada[l], b_ada[l]).reshape(B, 6, D)

    wg = _lru_gate_weights(lru_wa[l], lru_wx[l])
    qkv0, qkv1, qkv2, hg, ga, gr = _inproj(x, mod, w_in[l].astype(BF16), lru_conv_w[l], lru_conv_b[l], wg,
                                           lru_ba[l], lru_bx[l], lru_lambda[l])

    bias = _bias_tiles(rel_bias, _bucket_maps())
    o_attn = _attention(qkv0, qkv1, qkv2, bias)

    x1 = _mix(x, o_attn, hg, ga, gr, mod, w_proj_attn[l].astype(BF16), w_proj_lru[l].astype(BF16),
              w_out[l].astype(BF16), ln1_g[l], ln1_b[l])

    return _ffn(x1, mod, ffn_w_up[l].astype(BF16), ffn_conv_w[l], ffn_conv_b[l][None, :],
                ffn_w_down[l].astype(BF16), ln2_g[l], ln2_b[l])
```

```python
import functools
import math

import jax
import jax.numpy as jnp
from jax import lax
from jax.experimental import pallas as pl
from jax.experimental.pallas import tpu as pltpu

F32 = jnp.float32
BF16 = jnp.bfloat16

D_MODEL = 1024
HEAD_DIM = 64
HEADS_PER_GROUP = 8
DILATED_GROUPS = ((128, 1), (512, 4), (2048, 16))
N_GROUPS = len(DILATED_GROUPS)
GROUP_W = HEADS_PER_GROUP * HEAD_DIM
ATTN_QKV = N_GROUPS * GROUP_W
BLOCK = 128
LOG2E = math.log2(math.e)
NUM_BUCKETS = 32
MAX_EXACT = NUM_BUCKETS // 2
MAX_DISTANCE = 2048
LRU_WIDTH = D_MODEL
LRU_BLOCKS = 16
LRU_BLOCK_W = LRU_WIDTH // LRU_BLOCKS
LRU_CONV = 4
C_RGLRU = 8.0
D_FF = 3 * D_MODEL
FFN_CONV = 3
DEPTH = 1
ALPHA = (2.0 * DEPTH) ** 0.25
LN_EPS = 1e-5

LANES = 128
SUBLANES = 8
VMEM_LIMIT = 56 * 1024 * 1024
TM_IN = 512
TM_MIX = 1024
MIX_SPLIT = 2
TM_FFN = 512
LRU_SCAN_UNROLL = 4
FF_CHUNKS = (1536, 1536)
assert sum(FF_CHUNKS) == D_FF
LRU_GRP = 256
HALO = SUBLANES
PAIR_W = 2 * HEAD_DIM
N_PAIRS = GROUP_W // PAIR_W
SLABS = GROUP_W // LANES


def _cparams(sem):
    return pltpu.CompilerParams(dimension_semantics=sem, vmem_limit_bytes=VMEM_LIMIT)


def _resident(shape, index_map):
    return pl.BlockSpec(shape, index_map, pipeline_mode=pl.Buffered(1))


def _gelu_tanh(x):
    return 0.5 * x * (1.0 + jnp.tanh(math.sqrt(2.0 / math.pi) * (x + 0.044715 * (x * x * x))))


def _sigmoid(x):
    return 0.5 * jnp.tanh(0.5 * x) + 0.5


def _layer_norm(z, g, b):
    mu = jnp.mean(z, axis=-1, keepdims=True)
    zc = z - mu
    var = jnp.mean(zc * zc, axis=-1, keepdims=True)
    return zc * lax.rsqrt(var + LN_EPS) * g + b


def _ada_kernel(c_ref, w_ref, b_ref, o_ref):
    c = c_ref[...]
    ca = (c * _sigmoid(c)).astype(BF16)
    o_ref[...] = jnp.dot(ca, w_ref[...].astype(BF16), preferred_element_type=F32) + b_ref[...]


def _ada(c, w_ada, b_ada):
    B, D = c.shape
    N = w_ada.shape[1]
    tn = D
    return pl.pallas_call(
        _ada_kernel,
        grid=(N // tn,),
        in_specs=[pl.BlockSpec((B, D), lambda j: (0, 0)),
                  pl.BlockSpec((D, tn), lambda j: (0, j)),
                  pl.BlockSpec((1, tn), lambda j: (0, j))],
        out_specs=pl.BlockSpec((B, tn), lambda j: (0, j)),
        out_shape=jax.ShapeDtypeStruct((B, N), F32),
        compiler_params=_cparams(("arbitrary",)),
        name="ada",
    )(c, w_ada, b_ada.reshape(1, N))


def _inproj_kernel(x_ref, mod_ref, w_ref, cw_ref, cb_ref, wg_ref, ba_ref, bx_ref, lam_ref,
                   g0_ref, g1_ref, g2_ref, hg_ref, ga_ref, gr_ref,
                   u_sc, xb_sc, split_sc, xpad_sc, carry_sc, ta_sc, tb_sc):
    tm = x_ref.shape[0]
    W = LRU_WIDTH
    nseg = SUBLANES
    seg = tm // nseg
    nsl = W // LANES
    sh1 = mod_ref[0:1, :]
    sc1 = mod_ref[1:2, :]
    u_sc[...] = (x_ref[...] * (1.0 + sc1) + sh1).astype(BF16)

    def mm(c0, n):
        return jnp.dot(u_sc[...], w_ref[:, c0:c0 + n], preferred_element_type=F32)

    @pl.when(pl.program_id(1) == 0)
    def _():
        xpad_sc[0:HALO, :] = jnp.zeros((HALO, W), F32)
        carry_sc[...] = jnp.zeros(carry_sc.shape, F32)

    col_lru = 3 * ATTN_QKV
    xl = mm(col_lru, W)
    xpad_sc[HALO:HALO + tm, :] = xl
    xp = xpad_sc[...]
    t = cw_ref[0:1, :] * xp
    for kk in range(1, LRU_CONV):
        t = cw_ref[kk:kk + 1, :] * xp + pltpu.roll(t, 1, 0)
    xr = t[HALO:, :] + cb_ref[...]
    xpad_sc[0:HALO, :] = xl[tm - HALO:tm, :]

    n_split = [0]

    def store_group(g, which, res):
        ref = (g0_ref, g1_ref, g2_ref)[g]
        dil = DILATED_GROUPS[g][1]
        if dil == 1:
            ref[which] = res.astype(BF16)
            return
        buf = n_split[0] % split_sc.shape[0]
        n_split[0] += 1
        for s in range(SLABS):
            split_sc[buf, s] = res[:, s * LANES:(s + 1) * LANES]
        rows = tm // dil
        for r in range(dil):
            for s in range(SLABS):
                ref[which, r, :, s * LANES:(s + 1) * LANES] = (
                    split_sc[buf, s, pl.ds(r, rows, stride=dil), :].astype(BF16))

    def qkv(which):
        res = mm(which * ATTN_QKV, ATTN_QKV)
        if which == 0:
            res = res * (HEAD_DIM ** -0.5 * LOG2E)
        for g in range(N_GROUPS):
            store_group(g, which, res[:, g * GROUP_W:(g + 1) * GROUP_W])

    qkv(0)
    nl = -lam_ref[...]
    softplus = jnp.maximum(nl, 0.0) + jnp.log1p(jnp.exp(-jnp.abs(nl)))
    half = (0.5 * C_RGLRU) * softplus
    xb_sc[...] = xr.astype(BF16)
    zs = [jnp.dot(xb_sc[:, c * LRU_GRP:(c + 1) * LRU_GRP], wg_ref[c], preferred_element_type=F32)
          for c in range(W // LRU_GRP)]
    za = jnp.concatenate([z[:, :LRU_GRP] for z in zs], axis=1)
    zx = jnp.concatenate([z[:, LRU_GRP:] for z in zs], axis=1)
    gi = _sigmoid(zx + bx_ref[...])
    neg_log_a = half * jnp.tanh(0.5 * (za + ba_ref[...])) + half
    a = jnp.exp2(neg_log_a * (-LOG2E))
    gap = jnp.tanh(neg_log_a) * (a * a + 1.0)
    root = jnp.where(gap > 0.0, gap * lax.rsqrt(gap), 0.0)
    bt = root * (gi * xr)
    for c in range(nsl):
        for i in range(nseg):
            dst = pl.ds(i, seg, stride=nseg)
            ta_sc[c, dst, :] = a[i * seg:(i + 1) * seg, c * LANES:(c + 1) * LANES]
            tb_sc[c, dst, :] = bt[i * seg:(i + 1) * seg, c * LANES:(c + 1) * LANES]
    qkv(1)
    qkv(2)

    gate = _gelu_tanh(mm(col_lru + W, W)).astype(BF16)
    col = col_lru + 2 * W
    for ref in (ga_ref, gr_ref):
        ref[...] = _sigmoid(mm(col, D_MODEL)).astype(BF16)
        col += D_MODEL

    def step(j, hp):
        h, p = hp
        rows = pl.ds(pl.multiple_of(j * nseg, nseg), nseg)
        aj = ta_sc[:, rows, :]
        h = aj * h + tb_sc[:, rows, :]
        p = aj * p
        tb_sc[:, rows, :] = h
        ta_sc[:, rows, :] = p
        return h, p

    state = (jnp.zeros((nsl, nseg, LANES), F32), jnp.ones((nsl, nseg, LANES), F32))
    x_end, p_end = lax.fori_loop(0, seg, step, state, unroll=LRU_SCAN_UNROLL)

    sub = lax.broadcasted_iota(jnp.int32, (nsl, nseg, LANES), 1)
    sh = 1
    while sh < nseg:
        x_sh = jnp.where(sub >= sh, pltpu.roll(x_end, sh, 1), 0.0)
        p_sh = jnp.where(sub >= sh, pltpu.roll(p_end, sh, 1), 1.0)
        x_end = p_end * x_sh + x_end
        p_end = p_end * p_sh
        sh *= 2
    carry = carry_sc[...]
    leave = x_end + p_end * carry
    h_init = jnp.where(sub >= 1, pltpu.roll(leave, 1, 1), carry)
    carry_sc[...] = jnp.broadcast_to(leave[:, nseg - 1:nseg, :], carry_sc.shape)

    for c in range(nsl):
        for i in range(nseg):
            src = pl.ds(i, seg, stride=nseg)
            h = tb_sc[c, src, :] + ta_sc[c, src, :] * h_init[c, i:i + 1, :]
            g = gate[i * seg:(i + 1) * seg, c * LANES:(c + 1) * LANES].astype(F32)
            hg_ref[i * seg:(i + 1) * seg, c * LANES:(c + 1) * LANES] = (h * g).astype(BF16)


def _inproj(x, mod, w_in_b, conv_w, conv_b, wg, ba, bx, lam):
    B, S, D = x.shape
    tm = TM_IN
    W = LRU_WIDTH
    ncols = w_in_b.shape[1]
    row = lambda b, i: (b, i, 0)
    const2 = lambda b, i: (0, 0)
    vec = pl.BlockSpec((1, W), const2)
    bf_shape = jax.ShapeDtypeStruct((B, S, D), BF16)
    row_spec = pl.BlockSpec((None, tm, D), row)
    qkv_shapes, qkv_specs = [], []
    for _, dil in DILATED_GROUPS:
        if dil == 1:
            qkv_shapes.append(jax.ShapeDtypeStruct((3, B, S, GROUP_W), BF16))
            qkv_specs.append(pl.BlockSpec((3, None, tm, GROUP_W), lambda b, i: (0, b, i, 0)))
        else:
            qkv_shapes.append(jax.ShapeDtypeStruct((3, B, dil, S // dil, GROUP_W), BF16))
            qkv_specs.append(pl.BlockSpec((3, None, dil, tm // dil, GROUP_W), lambda b, i: (0, b, 0, i, 0)))
    return pl.pallas_call(
        _inproj_kernel,
        grid=(B, S // tm),
        in_specs=[row_spec,
                  pl.BlockSpec((None, 6, D), lambda b, i: (b, 0, 0)),
                  _resident((D, ncols), const2),
                  pl.BlockSpec((LRU_CONV, W), const2), vec,
                  _resident(wg.shape, lambda b, i: (0, 0, 0)), vec, vec, vec],
        out_specs=qkv_specs + [row_spec, row_spec, row_spec],
        out_shape=qkv_shapes + [bf_shape, bf_shape, bf_shape],
        scratch_shapes=[pltpu.VMEM((tm, D), BF16), pltpu.VMEM((tm, W), BF16),
                        pltpu.VMEM((2, SLABS, tm, LANES), F32),
                        pltpu.VMEM((tm + HALO, W), F32), pltpu.VMEM((W // LANES, SUBLANES, LANES), F32),
                        pltpu.VMEM((W // LANES, tm, LANES), F32), pltpu.VMEM((W // LANES, tm, LANES), F32)],
        compiler_params=_cparams(("parallel", "arbitrary")),
        name="inproj",
    )(x, mod, w_in_b, conv_w, conv_b.reshape(1, W), wg, ba.reshape(1, W), bx.reshape(1, W), lam.reshape(1, W))


def _bias_kernel(tab_ref, bucket_ref, o_ref):
    hh = pl.program_id(0)
    bucket = bucket_ref[...]
    acc = jnp.zeros(bucket.shape, F32)
    for b in range(NUM_BUCKETS):
        acc = jnp.where(bucket == b, tab_ref[b, hh], acc)
    qi = lax.broadcasted_iota(jnp.int32, bucket.shape, 0)
    kj = lax.broadcasted_iota(jnp.int32, bucket.shape, 1)
    dist = qi + BLOCK - kj
    band = (dist >= 0) & (dist <= BLOCK)
    o_ref[...] = jnp.where(band, acc * LOG2E, -jnp.inf)


def _bias_tiles(rel_bias, buckets):
    nh = rel_bias.shape[1]
    return pl.pallas_call(
        _bias_kernel,
        grid=(nh,),
        in_specs=[pl.BlockSpec(memory_space=pltpu.SMEM),
                  pl.BlockSpec((None, BLOCK, 2 * BLOCK), lambda h: (h // HEADS_PER_GROUP, 0, 0))],
        out_specs=pl.BlockSpec((None, BLOCK, 2 * BLOCK), lambda h: (h, 0, 0)),
        out_shape=jax.ShapeDtypeStruct((nh, BLOCK, 2 * BLOCK), F32),
        compiler_params=_cparams(("arbitrary",)),
        name="bias_tiles",
    )(rel_bias, buckets)


def _bucket_maps():
    qi = jnp.arange(BLOCK)[:, None]
    kj = jnp.arange(2 * BLOCK)[None, :]
    dist = jnp.maximum(qi + BLOCK - kj, 0)
    maps = []
    for _, dil in DILATED_GROUPS:
        d = dist * dil
        nf = jnp.maximum(d, 1).astype(F32)
        large = MAX_EXACT + (jnp.log(nf / MAX_EXACT) / math.log(MAX_DISTANCE / MAX_EXACT)
                             * (NUM_BUCKETS - MAX_EXACT)).astype(jnp.int32)
        large = jnp.minimum(large, NUM_BUCKETS - 1)
        maps.append(jnp.where(d < MAX_EXACT, d, large))
    return jnp.stack(maps, 0).astype(jnp.int32)


def _attn_kernel(g0_ref, g1_ref, g2_ref, bias_ref, o_ref, part_sc):
    hp = pl.program_id(1)
    lane = lax.broadcasted_iota(jnp.int32, (BLOCK, PAIR_W), 1)
    first_head = lane < HEAD_DIM

    refs = (g0_ref, g1_ref, g2_ref)
    blocks = [(g, r, n) for g in (1, 2) for r in range(DILATED_GROUPS[g][1])
              for n in range(refs[g].shape[2] // BLOCK)]
    blocks += [(0, 0, n) for n in range(g0_ref.shape[1] // BLOCK)]

    def rows_of(g, r, w):
        return refs[g].at[w] if g == 0 else refs[g].at[w, r]

    def logits_of(g, r, n, e):
        qp = rows_of(g, r, 0)[n * BLOCK:(n + 1) * BLOCK, :]
        k0 = max(n - 1, 0) * BLOCK
        kp = rows_of(g, r, 1)[k0:(n + 1) * BLOCK, :]
        zero = jnp.zeros_like(qp)
        qe = jnp.where(first_head, qp, zero) if e == 0 else jnp.where(first_head, zero, qp)
        bias = bias_ref[g * N_PAIRS + hp, e * BLOCK:(e + 1) * BLOCK, :]
        if n == 0:
            bias = bias[:, BLOCK:]
        return lax.dot_general(qe, kp, (((1,), (1,)), ((), ())), preferred_element_type=F32) + bias

    def softmax_of(logits):
        m = jnp.max(logits, axis=-1, keepdims=True)
        return m, jnp.exp2(logits - m).astype(BF16)

    def weighted_values(g, r, n, p):
        vp = rows_of(g, r, 2)[max(n - 1, 0) * BLOCK:(n + 1) * BLOCK, :]
        v1 = jnp.concatenate([vp, jnp.ones(vp.shape, BF16)], axis=1)
        return jnp.dot(p, v1, preferred_element_type=F32)

    def finish(g, r, n, heads):
        (ma, pva), (mb, pvb) = heads
        pair = lambda a, b: jnp.where(first_head, a, b)
        m0, s0, pv0 = pair(ma, mb), pair(pva[:, PAIR_W:], pvb[:, PAIR_W:]), pair(pva[:, :PAIR_W], pvb[:, :PAIR_W])
        if g > 0:
            dil = DILATED_GROUPS[g][1]
            for j, t in enumerate((m0, s0, pv0)):
                part_sc[3 * (g - 1) + j, pl.ds(n * BLOCK * dil + r, BLOCK, stride=dil), :] = t
            return
        rows = slice(n * BLOCK, (n + 1) * BLOCK)
        m1, s1, pv1 = part_sc[0, rows, :], part_sc[1, rows, :], part_sc[2, rows, :]
        m2, s2, pv2 = part_sc[3, rows, :], part_sc[4, rows, :], part_sc[5, rows, :]
        mx = jnp.maximum(jnp.maximum(m0, m1), m2)
        a0, a1, a2 = jnp.exp2(m0 - mx), jnp.exp2(m1 - mx), jnp.exp2(m2 - mx)
        den = a0 * s0 + a1 * s1 + a2 * s2
        num = a0 * pv0 + a1 * pv1 + a2 * pv2
        o_ref[rows, :] = (num / den).astype(o_ref.dtype)

    units = [(blk, e) for blk in blocks for e in range(2)]
    logits = logits_of(*units[0][0], units[0][1])
    heads = []
    for i, (blk, e) in enumerate(units):
        nxt = logits_of(*units[i + 1][0], units[i + 1][1]) if i + 1 < len(units) else None
        m, p = softmax_of(logits)
        heads.append((m, weighted_values(*blk, p)))
        if e == 1:
            finish(*blk, heads)
            heads = []
        logits = nxt


def _attention(qkv0, qkv1, qkv2, bias):
    _, B, S, _ = qkv0.shape
    specs = [pl.BlockSpec((3, None, S, PAIR_W), lambda b, h: (0, b, 0, h))]
    for t in (qkv1, qkv2):
        dil, L = t.shape[2], t.shape[3]
        specs.append(pl.BlockSpec((3, None, dil, L, PAIR_W), lambda b, h: (0, b, 0, 0, h)))
    bias2 = bias.reshape(N_GROUPS * N_PAIRS, 2 * BLOCK, 2 * BLOCK)
    specs.append(_resident(bias2.shape, lambda b, h: (0, 0, 0)))
    return pl.pallas_call(
        _attn_kernel,
        grid=(B, N_PAIRS),
        in_specs=specs,
        out_specs=pl.BlockSpec((None, S, PAIR_W), lambda b, h: (b, 0, h)),
        out_shape=jax.ShapeDtypeStruct((B, S, GROUP_W), BF16),
        scratch_shapes=[pltpu.VMEM((3 * (N_GROUPS - 1), S, LANES), F32)],
        compiler_params=_cparams(("parallel", "parallel")),
        name="attn",
    )(qkv0, qkv1, qkv2, bias2)


def _lru_gate_weights(wa, wx):
    per = LRU_GRP // LRU_BLOCK_W
    ngrp = LRU_BLOCKS // per

    def bd(w):
        w = w.reshape(ngrp, per, LRU_BLOCK_W, LRU_BLOCK_W)
        eye = jnp.eye(per, dtype=w.dtype)
        t = jnp.einsum('gpcd,pq->gpcqd', w, eye)
        return t.reshape(ngrp, LRU_GRP, LRU_GRP)

    return jnp.concatenate([bd(wa), bd(wx)], axis=2).astype(BF16)


def _mix_kernel(x_ref, oa_ref, hg_ref, ga_ref, gr_ref, mod_ref, wpa_ref, wpl_ref, wo_ref,
                g_ref, b_ref, o_ref):
    g1 = mod_ref[2:3, :]
    tm = x_ref.shape[0]
    sub = tm // MIX_SPLIT

    def merged_of(h):
        rows = slice(h * sub, (h + 1) * sub)
        ya = jnp.dot(oa_ref[rows, :], wpa_ref[...], preferred_element_type=F32)
        yl = jnp.dot(hg_ref[rows, :], wpl_ref[...], preferred_element_type=F32)
        return (ga_ref[rows, :].astype(F32) * ya + gr_ref[rows, :].astype(F32) * yl).astype(BF16)

    merged = merged_of(0)
    for h in range(MIX_SPLIT):
        nxt = merged_of(h + 1) if h + 1 < MIX_SPLIT else None
        rows = slice(h * sub, (h + 1) * sub)
        mo = jnp.dot(merged, wo_ref[...], preferred_element_type=F32)
        z = ALPHA * x_ref[rows, :] + g1 * mo
        o_ref[rows, :] = _layer_norm(z, g_ref[...], b_ref[...])
        merged = nxt


def _mix(x, o_attn, hg, ga, gr, mod, wpa, wpl, wo, ln_g, ln_b):
    B, S, D = x.shape
    tm = TM_MIX
    row = lambda b, i: (b, i, 0)
    const2 = lambda b, i: (0, 0)
    rspec = lambda w: pl.BlockSpec((None, tm, w), row)
    vec = pl.BlockSpec((1, D), const2)
    return pl.pallas_call(
        _mix_kernel,
        grid=(B, S // tm),
        in_specs=[rspec(D), rspec(GROUP_W), rspec(LRU_WIDTH), rspec(D), rspec(D),
                  pl.BlockSpec((None, 6, D), lambda b, i: (b, 0, 0)),
                  _resident(wpa.shape, const2), _resident(wpl.shape, const2), _resident(wo.shape, const2),
                  vec, vec],
        out_specs=rspec(D),
        out_shape=jax.ShapeDtypeStruct((B, S, D), F32),
        compiler_params=_cparams(("parallel", "parallel")),
        name="mix",
    )(x, o_attn, hg, ga, gr, mod, wpa, wpl, wo, ln_g.reshape(1, D), ln_b.reshape(1, D))


def _ffn_kernel(x_ref, mod_ref, wup_ref, cw_ref, cb_ref, wdn_ref, g_ref, b_ref, o_ref, perm_sc, carry_sc):
    tm = x_ref.shape[0]
    nseg = SUBLANES
    seg = tm // nseg
    nsl = D_MODEL // LANES
    back = FFN_CONV - 1

    @pl.when(pl.program_id(1) == 0)
    def _():
        carry_sc[...] = jnp.zeros(carry_sc.shape, F32)

    for c in range(nsl):
        for i in range(nseg):
            perm_sc[c, pl.ds(i, seg, stride=nseg), :] = x_ref[i * seg:(i + 1) * seg, c * LANES:(c + 1) * LANES]
    x1 = jnp.concatenate([perm_sc[c] for c in range(nsl)], axis=1)
    sh2 = mod_ref[3:4, :]
    sc2 = mod_ref[4:5, :]
    g2 = mod_ref[5:6, :]
    u2 = (x1 * (1.0 + sc2) + sh2).astype(BF16)

    def up(c0, c1):
        return jnp.dot(u2, wup_ref[:, c0:c1], preferred_element_type=F32)

    def conv(h, c0, c1):
        sub = lax.broadcasted_iota(jnp.int32, (nseg, c1 - c0), 0)
        head = []
        for v in range(back):
            rows = slice(tm - (back - v) * nseg, tm - (back - v - 1) * nseg)
            prev = carry_sc[v * nseg:(v + 1) * nseg, c0:c1]
            head.append(jnp.where(sub >= 1, pltpu.roll(h[rows, :], 1, 0), pltpu.roll(prev, 1, 0)))
            carry_sc[v * nseg:(v + 1) * nseg, c0:c1] = h[rows, :]
        hx = jnp.concatenate(head + [h], axis=0)
        y = cb_ref[:, c0:c1] + cw_ref[back:back + 1, c0:c1] * h
        for kk in range(back):
            y = y + cw_ref[kk:kk + 1, c0:c1] * hx[kk * nseg:kk * nseg + tm, :]
        return y

    bounds = [sum(FF_CHUNKS[:k]) for k in range(len(FF_CHUNKS) + 1)]
    ups = lambda k: (up(bounds[k], bounds[k + 1]), up(D_FF + bounds[k], D_FF + bounds[k + 1]))
    acc = jnp.zeros((tm, D_MODEL), F32)
    hv, hg = ups(0)
    for k in range(len(FF_CHUNKS)):
        nxt = ups(k + 1) if k + 1 < len(FF_CHUNKS) else None
        c0, c1 = bounds[k], bounds[k + 1]
        val = conv(hv, c0, c1)
        gate = conv(hg, D_FF + c0, D_FF + c1)
        act = (_gelu_tanh(gate) * val).astype(BF16)
        acc = acc + jnp.dot(act, wdn_ref[c0:c1, :], preferred_element_type=F32)
        if nxt is not None:
            hv, hg = nxt
    z = ALPHA * x1 + g2 * acc
    out = _layer_norm(z, g_ref[...], b_ref[...])
    for c in range(nsl):
        perm_sc[c] = out[:, c * LANES:(c + 1) * LANES]
    for c in range(nsl):
        for i in range(nseg):
            o_ref[i * seg:(i + 1) * seg, c * LANES:(c + 1) * LANES] = perm_sc[c, pl.ds(i, seg, stride=nseg), :]


def _ffn(x1, mod, wup, cw, cb, wdn, ln_g, ln_b):
    B, S, D = x1.shape
    tm = TM_FFN
    row = lambda b, i: (b, i, 0)
    const2 = lambda b, i: (0, 0)
    vec = pl.BlockSpec((1, D), const2)
    return pl.pallas_call(
        _ffn_kernel,
        grid=(B, S // tm),
        in_specs=[pl.BlockSpec((None, tm, D), row),
                  pl.BlockSpec((None, 6, D), lambda b, i: (b, 0, 0)),
                  _resident(wup.shape, const2), _resident(cw.shape, const2),
                  _resident(cb.shape, const2), _resident(wdn.shape, const2),
                  vec, vec],
        out_specs=pl.BlockSpec((None, tm, D), row),
        out_shape=jax.ShapeDtypeStruct((B, S, D), F32),
        scratch_shapes=[pltpu.VMEM((D // LANES, tm, LANES), F32),
                        pltpu.VMEM(((FFN_CONV - 1) * SUBLANES, 2 * D_FF), F32)],
        compiler_params=_cparams(("parallel", "arbitrary")),
        name="ffn",
    )(x1, mod, wup, cw, cb, wdn, ln_g.reshape(1, D), ln_b.reshape(1, D))


def kernel(x, c, w_ada, b_ada, w_in, rel_bias, lru_conv_w, lru_conv_b, lru_wa, lru_ba, lru_wx, lru_bx,
           lru_lambda, w_proj_attn, w_proj_lru, w_out, ln1_g, ln1_b, ffn_w_up, ffn_conv_w, ffn_conv_b,
           ffn_w_down, ln2_g, ln2_b):
    B, S, D = x.shape
    assert w_ada.shape[0] == DEPTH == 1 and D == D_MODEL
    assert S == DILATED_GROUPS[-1][1] * BLOCK and S % TM_IN == 0
    assert all(window // dil == BLOCK for window, dil in DILATED_GROUPS)
    l = 0
    mod = _ada(c, w_ada[l], b_ada[l]).reshape(B, 6, D)

    wg = _lru_gate_weights(lru_wa[l], lru_wx[l])
    qkv0, qkv1, qkv2, hg, ga, gr = _inproj(x, mod, w_in[l].astype(BF16), lru_conv_w[l], lru_conv_b[l], wg,
                                           lru_ba[l], lru_bx[l], lru_lambda[l])

    bias = _bias_tiles(rel_bias, _bucket_maps())
    o_attn = _attention(qkv0, qkv1, qkv2, bias)

    x1 = _mix(x, o_attn, hg, ga, gr, mod, w_proj_attn[l].astype(BF16), w_proj_lru[l].astype(BF16),
              w_out[l].astype(BF16), ln1_g[l], ln1_b[l])

    return _ffn(x1, mod, ffn_w_up[l].astype(BF16), ffn_conv_w[l], ffn_conv_b[l][None, :],
                ffn_w_down[l].astype(BF16), ln2_g[l], ln2_b[l])
```

```python
import math

import jax
import jax.numpy as jnp
from jax import lax
from jax.experimental import pallas as pl
from jax.experimental.pallas import tpu as pltpu

F32 = jnp.float32
BF16 = jnp.bfloat16

D_MODEL = 1024
HEAD_DIM = 64
HEADS_PER_GROUP = 8
DILATED_GROUPS = ((128, 1), (512, 4), (2048, 16))
N_GROUPS = len(DILATED_GROUPS)
GROUP_W = HEADS_PER_GROUP * HEAD_DIM
ATTN_QKV = N_GROUPS * GROUP_W
BLOCK = 128
LOG2E = math.log2(math.e)
NUM_BUCKETS = 32
MAX_EXACT = NUM_BUCKETS // 2
MAX_DISTANCE = 2048
LRU_WIDTH = D_MODEL
LRU_BLOCKS = 16
LRU_BLOCK_W = LRU_WIDTH // LRU_BLOCKS
LRU_CONV = 4
C_RGLRU = 8.0
D_FF = 3 * D_MODEL
FFN_CONV = 3
DEPTH = 1
ALPHA = (2.0 * DEPTH) ** 0.25
LN_EPS = 1e-5

LANES = 128
SUBLANES = 8
VMEM_LIMIT = 56 * 1024 * 1024
TM_IN = 512
TM_MIX = 1024
MIX_SPLIT = 2
TM_FFN = 512
LRU_SCAN_UNROLL = 4
FF_CHUNKS = (1536, 1536)
assert sum(FF_CHUNKS) == D_FF
LRU_GRP = 256
HALO = SUBLANES
PAIR_W = 2 * HEAD_DIM
N_PAIRS = GROUP_W // PAIR_W
SLABS = GROUP_W // LANES


def _cparams(sem):
    return pltpu.CompilerParams(dimension_semantics=sem, vmem_limit_bytes=VMEM_LIMIT)


def _resident(shape, index_map):
    return pl.BlockSpec(shape, index_map, pipeline_mode=pl.Buffered(1))


def _gelu_tanh(x):
    return 0.5 * x * (1.0 + jnp.tanh(math.sqrt(2.0 / math.pi) * (x + 0.044715 * (x * x * x))))


def _sigmoid(x):
    return 0.5 * jnp.tanh(0.5 * x) + 0.5


def _layer_norm(z, g, b):
    mu = jnp.mean(z, axis=-1, keepdims=True)
    zc = z - mu
    var = jnp.mean(zc * zc, axis=-1, keepdims=True)
    return zc * lax.rsqrt(var + LN_EPS) * g + b


def _ada_kernel(c_ref, w_ref, b_ref, o_ref):
    c = c_ref[...]
    ca = (c * _sigmoid(c)).astype(BF16)
    o_ref[...] = jnp.dot(ca, w_ref[...].astype(BF16), preferred_element_type=F32) + b_ref[...]


def _ada(c, w_ada, b_ada):
    B, D = c.shape
    N = w_ada.shape[1]
    tn = D
    return pl.pallas_call(
        _ada_kernel,
        grid=(N // tn,),
        in_specs=[pl.BlockSpec((B, D), lambda j: (0, 0)),
                  pl.BlockSpec((D, tn), lambda j: (0, j)),
                  pl.BlockSpec((1, tn), lambda j: (0, j))],
        out_specs=pl.BlockSpec((B, tn), lambda j: (0, j)),
        out_shape=jax.ShapeDtypeStruct((B, N), F32),
        compiler_params=_cparams(("arbitrary",)),
        name="ada",
    )(c, w_ada, b_ada.reshape(1, N))


def _inproj_kernel(x_ref, mod_ref, w_ref, cw_ref, cb_ref, wg_ref, ba_ref, bx_ref, lam_ref,
                   g0_ref, g1_ref, g2_ref, hg_ref, ga_ref, gr_ref,
                   u_sc, xb_sc, split_sc, xpad_sc, carry_sc, ta_sc, tb_sc):
    tm = x_ref.shape[0]
    W = LRU_WIDTH
    nseg = SUBLANES
    seg = tm // nseg
    nsl = W // LANES
    sh1 = mod_ref[0:1, :]
    sc1 = mod_ref[1:2, :]
    u_sc[...] = (x_ref[...] * (1.0 + sc1) + sh1).astype(BF16)

    def mm(c0, n):
        return jnp.dot(u_sc[...], w_ref[:, c0:c0 + n], preferred_element_type=F32)

    @pl.when(pl.program_id(1) == 0)
    def _():
        xpad_sc[0:HALO, :] = jnp.zeros((HALO, W), F32)
        carry_sc[...] = jnp.zeros(carry_sc.shape, F32)

    col_lru = 3 * ATTN_QKV
    xl = mm(col_lru, W)
    xpad_sc[HALO:HALO + tm, :] = xl
    xp = xpad_sc[...]
    t = cw_ref[0:1, :] * xp
    for kk in range(1, LRU_CONV):
        t = cw_ref[kk:kk + 1, :] * xp + pltpu.roll(t, 1, 0)
    xr = t[HALO:, :] + cb_ref[...]
    xpad_sc[0:HALO, :] = xl[tm - HALO:tm, :]

    n_split = [0]

    def store_group(g, which, res):
        ref = (g0_ref, g1_ref, g2_ref)[g]
        dil = DILATED_GROUPS[g][1]
        if dil == 1:
            ref[which] = res.astype(BF16)
            return
        buf = n_split[0] % split_sc.shape[0]
        n_split[0] += 1
        for s in range(SLABS):
            split_sc[buf, s] = res[:, s * LANES:(s + 1) * LANES]
        rows = tm // dil
        for r in range(dil):
            for s in range(SLABS):
                ref[which, r, :, s * LANES:(s + 1) * LANES] = (
                    split_sc[buf, s, pl.ds(r, rows, stride=dil), :].astype(BF16))

    def qkv(which):
        res = mm(which * ATTN_QKV, ATTN_QKV)
        if which == 0:
            res = res * (HEAD_DIM ** -0.5 * LOG2E)
        for g in range(N_GROUPS):
            store_group(g, which, res[:, g * GROUP_W:(g + 1) * GROUP_W])

    qkv(0)
    nl = -lam_ref[...]
    softplus = jnp.maximum(nl, 0.0) + jnp.log1p(jnp.exp(-jnp.abs(nl)))
    half = (0.5 * C_RGLRU) * softplus
    xb_sc[...] = xr.astype(BF16)
    zs = [jnp.dot(xb_sc[:, c * LRU_GRP:(c + 1) * LRU_GRP], wg_ref[c], preferred_element_type=F32)
          for c in range(W // LRU_GRP)]
    za = jnp.concatenate([z[:, :LRU_GRP] for z in zs], axis=1)
    zx = jnp.concatenate([z[:, LRU_GRP:] for z in zs], axis=1)
    gi = _sigmoid(zx + bx_ref[...])
    neg_log_a = half * jnp.tanh(0.5 * (za + ba_ref[...])) + half
    a = jnp.exp2(neg_log_a * (-LOG2E))
    gap = jnp.tanh(neg_log_a) * (a * a + 1.0)
    root = jnp.where(gap > 0.0, gap * lax.rsqrt(gap), 0.0)
    bt = root * (gi * xr)
    for c in range(nsl):
        for i in range(nseg):
            dst = pl.ds(i, seg, stride=nseg)
            ta_sc[c, dst, :] = a[i * seg:(i + 1) * seg, c * LANES:(c + 1) * LANES]
            tb_sc[c, dst, :] = bt[i * seg:(i + 1) * seg, c * LANES:(c + 1) * LANES]
    qkv(1)
    qkv(2)

    gate = _gelu_tanh(mm(col_lru + W, W)).astype(BF16)
    col = col_lru + 2 * W
    for ref in (ga_ref, gr_ref):
        ref[...] = _sigmoid(mm(col, D_MODEL)).astype(BF16)
        col += D_MODEL

    def step(j, hp):
        h, p = hp
        rows = pl.ds(pl.multiple_of(j * nseg, nseg), nseg)
        aj = ta_sc[:, rows, :]
        h = aj * h + tb_sc[:, rows, :]
        p = aj * p
        tb_sc[:, rows, :] = h
        ta_sc[:, rows, :] = p
        return h, p

    state = (jnp.zeros((nsl, nseg, LANES), F32), jnp.ones((nsl, nseg, LANES), F32))
    x_end, p_end = lax.fori_loop(0, seg, step, state, unroll=LRU_SCAN_UNROLL)

    sub = lax.broadcasted_iota(jnp.int32, (nsl, nseg, LANES), 1)
    sh = 1
    while sh < nseg:
        x_sh = jnp.where(sub >= sh, pltpu.roll(x_end, sh, 1), 0.0)
        p_sh = jnp.where(sub >= sh, pltpu.roll(p_end, sh, 1), 1.0)
        x_end = p_end * x_sh + x_end
        p_end = p_end * p_sh
        sh *= 2
    carry = carry_sc[...]
    leave = x_end + p_end * carry
    h_init = jnp.where(sub >= 1, pltpu.roll(leave, 1, 1), carry)
    carry_sc[...] = jnp.broadcast_to(leave[:, nseg - 1:nseg, :], carry_sc.shape)

    for c in range(nsl):
        for i in range(nseg):
            src = pl.ds(i, seg, stride=nseg)
            h = tb_sc[c, src, :] + ta_sc[c, src, :] * h_init[c, i:i + 1, :]
            g = gate[i * seg:(i + 1) * seg, c * LANES:(c + 1) * LANES].astype(F32)
            hg_ref[i * seg:(i + 1) * seg, c * LANES:(c + 1) * LANES] = (h * g).astype(BF16)


def _inproj(x, mod, w_in_b, conv_w, conv_b, wg, ba, bx, lam):
    B, S, D = x.shape
    tm = TM_IN
    W = LRU_WIDTH
    ncols = w_in_b.shape[1]
    row = lambda b, i: (b, i, 0)
    const2 = lambda b, i: (0, 0)
    vec = pl.BlockSpec((1, W), const2)
    bf_shape = jax.ShapeDtypeStruct((B, S, D), BF16)
    row_spec = pl.BlockSpec((None, tm, D), row)
    qkv_shapes, qkv_specs = [], []
    for _, dil in DILATED_GROUPS:
        if dil == 1:
            qkv_shapes.append(jax.ShapeDtypeStruct((3, B, S, GROUP_W), BF16))
            qkv_specs.append(pl.BlockSpec((3, None, tm, GROUP_W), lambda b, i: (0, b, i, 0)))
        else:
            qkv_shapes.append(jax.ShapeDtypeStruct((3, B, dil, S // dil, GROUP_W), BF16))
            qkv_specs.append(pl.BlockSpec((3, None, dil, tm // dil, GROUP_W), lambda b, i: (0, b, 0, i, 0)))
    return pl.pallas_call(
        _inproj_kernel,
        grid=(B, S // tm),
        in_specs=[row_spec,
                  pl.BlockSpec((None, 6, D), lambda b, i: (b, 0, 0)),
                  _resident((D, ncols), const2),
                  pl.BlockSpec((LRU_CONV, W), const2), vec,
                  _resident(wg.shape, lambda b, i: (0, 0, 0)), vec, vec, vec],
        out_specs=qkv_specs + [row_spec, row_spec, row_spec],
        out_shape=qkv_shapes + [bf_shape, bf_shape, bf_shape],
        scratch_shapes=[pltpu.VMEM((tm, D), BF16), pltpu.VMEM((tm, W), BF16),
                        pltpu.VMEM((2, SLABS, tm, LANES), F32),
                        pltpu.VMEM((tm + HALO, W), F32), pltpu.VMEM((W // LANES, SUBLANES, LANES), F32),
                        pltpu.VMEM((W // LANES, tm, LANES), F32), pltpu.VMEM((W // LANES, tm, LANES), F32)],
        compiler_params=_cparams(("parallel", "arbitrary")),
        name="inproj",
    )(x, mod, w_in_b, conv_w, conv_b.reshape(1, W), wg, ba.reshape(1, W), bx.reshape(1, W), lam.reshape(1, W))


def _bias_kernel(tab_ref, bucket_ref, o_ref):
    hh = pl.program_id(0)
    bucket = bucket_ref[...]
    acc = jnp.zeros(bucket.shape, F32)
    for b in range(NUM_BUCKETS):
        acc = jnp.where(bucket == b, tab_ref[b, hh], acc)
    qi = lax.broadcasted_iota(jnp.int32, bucket.shape, 0)
    kj = lax.broadcasted_iota(jnp.int32, bucket.shape, 1)
    dist = qi + BLOCK - kj
    band = (dist >= 0) & (dist <= BLOCK)
    o_ref[...] = jnp.where(band, acc * LOG2E, -jnp.inf)


def _bias_tiles(rel_bias, buckets):
    nh = rel_bias.shape[1]
    return pl.pallas_call(
        _bias_kernel,
        grid=(nh,),
        in_specs=[pl.BlockSpec(memory_space=pltpu.SMEM),
                  pl.BlockSpec((None, BLOCK, 2 * BLOCK), lambda h: (h // HEADS_PER_GROUP, 0, 0))],
        out_specs=pl.BlockSpec((None, BLOCK, 2 * BLOCK), lambda h: (h, 0, 0)),
        out_shape=jax.ShapeDtypeStruct((nh, BLOCK, 2 * BLOCK), F32),
        compiler_params=_cparams(("arbitrary",)),
        name="bias_tiles",
    )(rel_bias, buckets)


def _bucket_maps():
    qi = jnp.arange(BLOCK)[:, None]
    kj = jnp.arange(2 * BLOCK)[None, :]
    dist = jnp.maximum(qi + BLOCK - kj, 0)
    maps = []
    for _, dil in DILATED_GROUPS:
        d = dist * dil
        nf = jnp.maximum(d, 1).astype(F32)
        large = MAX_EXACT + (jnp.log(nf / MAX_EXACT) / math.log(MAX_DISTANCE / MAX_EXACT)
                             * (NUM_BUCKETS - MAX_EXACT)).astype(jnp.int32)
        large = jnp.minimum(large, NUM_BUCKETS - 1)
        maps.append(jnp.where(d < MAX_EXACT, d, large))
    return jnp.stack(maps, 0).astype(jnp.int32)


def _attn_kernel(g0_ref, g1_ref, g2_ref, bias_ref, o_ref, part_sc):
    hp = pl.program_id(1)
    lane = lax.broadcasted_iota(jnp.int32, (BLOCK, PAIR_W), 1)
    first_head = lane < HEAD_DIM

    refs = (g0_ref, g1_ref, g2_ref)
    blocks = [(g, r, n) for g in (1, 2) for r in range(DILATED_GROUPS[g][1])
              for n in range(refs[g].shape[2] // BLOCK)]
    blocks += [(0, 0, n) for n in range(g0_ref.shape[1] // BLOCK)]

    def rows_of(g, r, w):
        return refs[g].at[w] if g == 0 else refs[g].at[w, r]

    def logits_of(g, r, n, e):
        qp = rows_of(g, r, 0)[n * BLOCK:(n + 1) * BLOCK, :]
        k0 = max(n - 1, 0) * BLOCK
        kp = rows_of(g, r, 1)[k0:(n + 1) * BLOCK, :]
        zero = jnp.zeros_like(qp)
        qe = jnp.where(first_head, qp, zero) if e == 0 else jnp.where(first_head, zero, qp)
        bias = bias_ref[g * N_PAIRS + hp, e * BLOCK:(e + 1) * BLOCK, :]
        if n == 0:
            bias = bias[:, BLOCK:]
        return lax.dot_general(qe, kp, (((1,), (1,)), ((), ())), preferred_element_type=F32) + bias

    def softmax_of(logits):
        m = jnp.max(logits, axis=-1, keepdims=True)
        return m, jnp.exp2(logits - m).astype(BF16)

    def weighted_values(g, r, n, p):
        vp = rows_of(g, r, 2)[max(n - 1, 0) * BLOCK:(n + 1) * BLOCK, :]
        v1 = jnp.concatenate([vp, jnp.ones(vp.shape, BF16)], axis=1)
        return jnp.dot(p, v1, preferred_element_type=F32)

    def finish(g, r, n, heads):
        (ma, pva), (mb, pvb) = heads
        pair = lambda a, b: jnp.where(first_head, a, b)
        m0, s0, pv0 = pair(ma, mb), pair(pva[:, PAIR_W:], pvb[:, PAIR_W:]), pair(pva[:, :PAIR_W], pvb[:, :PAIR_W])
        if g > 0:
            dil = DILATED_GROUPS[g][1]
            for j, t in enumerate((m0, s0, pv0)):
                part_sc[3 * (g - 1) + j, pl.ds(n * BLOCK * dil + r, BLOCK, stride=dil), :] = t
            return
        rows = slice(n * BLOCK, (n + 1) * BLOCK)
        m1, s1, pv1 = part_sc[0, rows, :], part_sc[1, rows, :], part_sc[2, rows, :]
        m2, s2, pv2 = part_sc[3, rows, :], part_sc[4, rows, :], part_sc[5, rows, :]
        mx = jnp.maximum(jnp.maximum(m0, m1), m2)
        a0, a1, a2 = jnp.exp2(m0 - mx), jnp.exp2(m1 - mx), jnp.exp2(m2 - mx)
        den = a0 * s0 + a1 * s1 + a2 * s2
        num = a0 * pv0 + a1 * pv1 + a2 * pv2
        o_ref[rows, :] = (num / den).astype(o_ref.dtype)

    units = [(blk, e) for blk in blocks for e in range(2)]
    logits = logits_of(*units[0][0], units[0][1])
    heads = []
    for i, (blk, e) in enumerate(units):
        nxt = logits_of(*units[i + 1][0], units[i + 1][1]) if i + 1 < len(units) else None
        m, p = softmax_of(logits)
        heads.append((m, weighted_values(*blk, p)))
        if e == 1:
            finish(*blk, heads)
            heads = []
        logits = nxt


def _attention(qkv0, qkv1, qkv2, bias):
    _, B, S, _ = qkv0.shape
    specs = [pl.BlockSpec((3, None, S, PAIR_W), lambda b, h: (0, b, 0, h))]
    for t in (qkv1, qkv2):
        dil, L = t.shape[2], t.shape[3]
        specs.append(pl.BlockSpec((3, None, dil, L, PAIR_W), lambda b, h: (0, b, 0, 0, h)))
    bias2 = bias.reshape(N_GROUPS * N_PAIRS, 2 * BLOCK, 2 * BLOCK)
    specs.append(_resident(bias2.shape, lambda b, h: (0, 0, 0)))
    return pl.pallas_call(
        _attn_kernel,
        grid=(B, N_PAIRS),
        in_specs=specs,
        out_specs=pl.BlockSpec((None, S, PAIR_W), lambda b, h: (b, 0, h)),
        out_shape=jax.ShapeDtypeStruct((B, S, GROUP_W), BF16),
        scratch_shapes=[pltpu.VMEM((3 * (N_GROUPS - 1), S, LANES), F32)],
        compiler_params=_cparams(("parallel", "parallel")),
        name="attn",
    )(qkv0, qkv1, qkv2, bias2)


def _lru_gate_weights(wa, wx):
    per = LRU_GRP // LRU_BLOCK_W
    ngrp = LRU_BLOCKS // per

    def bd(w):
        w = w.reshape(ngrp, per, LRU_BLOCK_W, LRU_BLOCK_W)
        eye = jnp.eye(per, dtype=w.dtype)
        t = jnp.einsum('gpcd,pq->gpcqd', w, eye)
        return t.reshape(ngrp, LRU_GRP, LRU_GRP)

    return jnp.concatenate([bd(wa), bd(wx)], axis=2).astype(BF16)


def _mix_kernel(x_ref, oa_ref, hg_ref, ga_ref, gr_ref, mod_ref, wpa_ref, wpl_ref, wo_ref,
                g_ref, b_ref, o_ref):
    g1 = mod_ref[2:3, :]
    tm = x_ref.shape[0]
    sub = tm // MIX_SPLIT
    for h in range(MIX_SPLIT):
        rows = slice(h * sub, (h + 1) * sub)
        ya = jnp.dot(oa_ref[rows, :], wpa_ref[...], preferred_element_type=F32)
        yl = jnp.dot(hg_ref[rows, :], wpl_ref[...], preferred_element_type=F32)
        merged = ga_ref[rows, :].astype(F32) * ya + gr_ref[rows, :].astype(F32) * yl
        mo = jnp.dot(merged.astype(BF16), wo_ref[...], preferred_element_type=F32)
        z = ALPHA * x_ref[rows, :] + g1 * mo
        o_ref[rows, :] = _layer_norm(z, g_ref[...], b_ref[...])


def _mix(x, o_attn, hg, ga, gr, mod, wpa, wpl, wo, ln_g, ln_b):
    B, S, D = x.shape
    tm = TM_MIX
    row = lambda b, i: (b, i, 0)
    const2 = lambda b, i: (0, 0)
    rspec = lambda w: pl.BlockSpec((None, tm, w), row)
    vec = pl.BlockSpec((1, D), const2)
    return pl.pallas_call(
        _mix_kernel,
        grid=(B, S // tm),
        in_specs=[rspec(D), rspec(GROUP_W), rspec(LRU_WIDTH), rspec(D), rspec(D),
                  pl.BlockSpec((None, 6, D), lambda b, i: (b, 0, 0)),
                  _resident(wpa.shape, const2), _resident(wpl.shape, const2), _resident(wo.shape, const2),
                  vec, vec],
        out_specs=rspec(D),
        out_shape=jax.ShapeDtypeStruct((B, S, D), F32),
        compiler_params=_cparams(("parallel", "parallel")),
        name="mix",
    )(x, o_attn, hg, ga, gr, mod, wpa, wpl, wo, ln_g.reshape(1, D), ln_b.reshape(1, D))


def _ffn_kernel(x_ref, mod_ref, wup_ref, cw_ref, cb_ref, wdn_ref, g_ref, b_ref, o_ref, perm_sc, carry_sc):
    tm = x_ref.shape[0]
    nseg = SUBLANES
    seg = tm // nseg
    nsl = D_MODEL // LANES
    back = FFN_CONV - 1

    @pl.when(pl.program_id(1) == 0)
    def _():
        carry_sc[...] = jnp.zeros(carry_sc.shape, F32)

    for c in range(nsl):
        for i in range(nseg):
            perm_sc[c, pl.ds(i, seg, stride=nseg), :] = x_ref[i * seg:(i + 1) * seg, c * LANES:(c + 1) * LANES]
    x1 = jnp.concatenate([perm_sc[c] for c in range(nsl)], axis=1)
    sh2 = mod_ref[3:4, :]
    sc2 = mod_ref[4:5, :]
    g2 = mod_ref[5:6, :]
    u2 = (x1 * (1.0 + sc2) + sh2).astype(BF16)

    def up(c0, c1):
        return jnp.dot(u2, wup_ref[:, c0:c1], preferred_element_type=F32)

    def conv(h, c0, c1):
        sub = lax.broadcasted_iota(jnp.int32, (nseg, c1 - c0), 0)
        head = []
        for v in range(back):
            rows = slice(tm - (back - v) * nseg, tm - (back - v - 1) * nseg)
            prev = carry_sc[v * nseg:(v + 1) * nseg, c0:c1]
            head.append(jnp.where(sub >= 1, pltpu.roll(h[rows, :], 1, 0), pltpu.roll(prev, 1, 0)))
            carry_sc[v * nseg:(v + 1) * nseg, c0:c1] = h[rows, :]
        hx = jnp.concatenate(head + [h], axis=0)
        y = cb_ref[:, c0:c1] + cw_ref[back:back + 1, c0:c1] * h
        for kk in range(back):
            y = y + cw_ref[kk:kk + 1, c0:c1] * hx[kk * nseg:kk * nseg + tm, :]
        return y

    acc = jnp.zeros((tm, D_MODEL), F32)
    c0 = 0
    for width in FF_CHUNKS:
        c1 = c0 + width
        val = conv(up(c0, c1), c0, c1)
        gate = conv(up(D_FF + c0, D_FF + c1), D_FF + c0, D_FF + c1)
        act = (_gelu_tanh(gate) * val).astype(BF16)
        acc = acc + jnp.dot(act, wdn_ref[c0:c1, :], preferred_element_type=F32)
        c0 = c1
    z = ALPHA * x1 + g2 * acc
    out = _layer_norm(z, g_ref[...], b_ref[...])
    for c in range(nsl):
        perm_sc[c] = out[:, c * LANES:(c + 1) * LANES]
    for c in range(nsl):
        for i in range(nseg):
            o_ref[i * seg:(i + 1) * seg, c * LANES:(c + 1) * LANES] = perm_sc[c, pl.ds(i, seg, stride=nseg), :]


def _ffn(x1, mod, wup, cw, cb, wdn, ln_g, ln_b):
    B, S, D = x1.shape
    tm = TM_FFN
    row = lambda b, i: (b, i, 0)
    const2 = lambda b, i: (0, 0)
    vec = pl.BlockSpec((1, D), const2)
    return pl.pallas_call(
        _ffn_kernel,
        grid=(B, S // tm),
        in_specs=[pl.BlockSpec((None, tm, D), row),
                  pl.BlockSpec((None, 6, D), lambda b, i: (b, 0, 0)),
                  _resident(wup.shape, const2), _resident(cw.shape, const2),
                  _resident(cb.shape, const2), _resident(wdn.shape, const2),
                  vec, vec],
        out_specs=pl.BlockSpec((None, tm, D), row),
        out_shape=jax.ShapeDtypeStruct((B, S, D), F32),
        scratch_shapes=[pltpu.VMEM((D // LANES, tm, LANES), F32),
                        pltpu.VMEM(((FFN_CONV - 1) * SUBLANES, 2 * D_FF), F32)],
        compiler_params=_cparams(("parallel", "arbitrary")),
        name="ffn",
    )(x1, mod, wup, cw, cb, wdn, ln_g.reshape(1, D), ln_b.reshape(1, D))


def kernel(x, c, w_ada, b_ada, w_in, rel_bias, lru_conv_w, lru_conv_b, lru_wa, lru_ba, lru_wx, lru_bx,
           lru_lambda, w_proj_attn, w_proj_lru, w_out, ln1_g, ln1_b, ffn_w_up, ffn_conv_w, ffn_conv_b,
           ffn_w_down, ln2_g, ln2_b):
    B, S, D = x.shape
    assert w_ada.shape[0] == DEPTH == 1 and D == D_MODEL
    assert S == DILATED_GROUPS[-1][1] * BLOCK and S % TM_IN == 0
    assert all(window // dil == BLOCK for window, dil in DILATED_GROUPS)
    l = 0
    mod = _ada(c, w_ada[l], b_ada[l]).reshape(B, 6, D)

    wg = _lru_gate_weights(lru_wa[l], lru_wx[l])
    qkv0, qkv1, qkv2, hg, ga, gr = _inproj(x, mod, w_in[l].astype(BF16), lru_conv_w[l], lru_conv_b[l], wg,
                                           lru_ba[l], lru_bx[l], lru_lambda[l])

    bias = _bias_tiles(rel_bias, _bucket_maps())
    o_attn = _attention(qkv0, qkv1, qkv2, bias)

    x1 = _mix(x, o_attn, hg, ga, gr, mod, w_proj_attn[l].astype(BF16), w_proj_lru[l].astype(BF16),
              w_out[l].astype(BF16), ln1_g[l], ln1_b[l])

    return _ffn(x1, mod, ffn_w_up[l].astype(BF16), ffn_conv_w[l], ffn_conv_b[l][None, :],
                ffn_w_down[l].astype(BF16), ln2_g[l], ln2_b[l])
```
